```python
import jax, jax.numpy as jnp
from jax import lax
import numpy as np

D_MODEL = 1024
BATCH = 16
SEQ = 4096
DEPTH = 1
DEC_BATCH = 2
DEC_SEQ = 8192
PAST_LEN = 128

HG_WIDTH = D_MODEL // 2
HG_HEAD_DIM = 128
HG_HEADS = HG_WIDTH // HG_HEAD_DIM
HG_CHUNK = 32
RW_WIDTH = D_MODEL - HG_WIDTH
RW_HEAD_DIM = 64
RW_HEADS = RW_WIDTH // RW_HEAD_DIM
RW_DECAY_RANK = 64
RW_AAA_RANK = 64
RW_GATE_RANK = 128
HG_COLS = 5 * HG_WIDTH
RW_COLS = 3 * RW_WIDTH + RW_DECAY_RANK + RW_AAA_RANK + RW_GATE_RANK
RW_SPLITS = (RW_WIDTH, 2 * RW_WIDTH, 3 * RW_WIDTH, 3 * RW_WIDTH + RW_DECAY_RANK,
             3 * RW_WIDTH + RW_DECAY_RANK + RW_AAA_RANK)
IN_COLS = HG_COLS + RW_COLS
N_EXPERTS = 32
TOP_K = 4
D_FF = D_MODEL
SWIGLU_LIMIT = 7.0
SWIGLU_ALPHA = 1.702
MOE_BLOCK = 128
NORM_EPS = 1e-5
RW_LN_EPS = 64e-5

kernel_name = "hgrn2_rwkv7_moe_bidir_encoder"


def rmsnorm(x, g):
    xf = x.astype(jnp.float32)
    y = xf * lax.rsqrt(jnp.mean(xf * xf, axis=-1, keepdims=True) + NORM_EPS)
    return (y * g.astype(jnp.float32)).astype(x.dtype)


def heads(a, n_heads):
    return a.reshape(a.shape[:-1] + (n_heads, a.shape[-1] // n_heads))


def centred_shift(x):
    xp = jnp.pad(x, ((0, 0), (1, 1), (0, 0)))
    return 0.5 * (xp[:, :-2] + xp[:, 2:])


def bidir(a_fwd, a_bwd):
    return jnp.stack([a_fwd, jnp.flip(a_bwd, axis=1)])


def merge_dirs(o):
    return o[0] + jnp.flip(o[1], axis=1)


def hgrn2_bidir_scan(q, k, v, log_f):
    P, Bsz, T, H, DK = q.shape
    DV = v.shape[-1]
    nc = T // HG_CHUNK

    def to_chunks(a):
        return jnp.moveaxis(a.reshape(P, Bsz, nc, HG_CHUNK, H, a.shape[-1]), 2, 0)

    incl = jnp.tril(jnp.ones((HG_CHUNK, HG_CHUNK), dtype=bool))[:, :, None, None]

    def chunk_step(S, blk):
        qc, kc, vc, fc = blk
        b = jnp.cumsum(fc, axis=2)
        rel = b[:, :, :, None] - b[:, :, None, :]
        decay = jnp.exp(jnp.where(incl, rel, -jnp.inf))
        scores = jnp.einsum('pbtshk,pbshk->pbhts', qc[:, :, :, None] * decay, kc)
        o = (jnp.einsum('pbhts,pbshv->pbthv', scores, vc)
             + jnp.einsum('pbthk,pbhkv->pbthv', qc * jnp.exp(b), S))
        b_end = b[:, :, -1]
        S = (jnp.exp(b_end)[..., None] * S
             + jnp.einsum('pbshk,pbshv->pbhkv', kc * jnp.exp(b_end[:, :, None] - b), vc))
        return S, o

    S0 = jnp.zeros((P, Bsz, H, DK, DV), jnp.float32)
    _, o = lax.scan(chunk_step, S0, (to_chunks(q), to_chunks(k), to_chunks(v), to_chunks(log_f)))
    return jnp.moveaxis(o, 0, 2).reshape(P, Bsz, T, H, DV)


def hgrn2_mixer(proj, lower_bound, norm_g):
    q, f_fwd, f_bwd, i, g = jnp.split(proj.astype(jnp.float32), 5, axis=-1)
    f = lower_bound + (1.0 - lower_bound) * jax.nn.sigmoid(bidir(f_fwd, f_bwd))
    q = jax.nn.silu(q)
    o = hgrn2_bidir_scan(heads(bidir(q, q), HG_HEADS), heads(1.0 - f, HG_HEADS),
                         heads(bidir(i, i), HG_HEADS), heads(jnp.log(f), HG_HEADS))
    o = merge_dirs(o)
    o = o * lax.rsqrt(jnp.mean(o * o, axis=-1, keepdims=True) + NORM_EPS) * heads(norm_g.astype(jnp.float32), HG_HEADS)
    return o.reshape(proj.shape[:-1] + (HG_WIDTH,)) * jax.nn.silu(g)


def rwkv7_step(S, inp):
    r, w, k, v, a, b = inp
    sa = jnp.einsum('pbhvk,pbhk->pbhv', S, a)
    S = S * w[:, :, :, None, :] + sa[..., None] * b[:, :, :, None, :] + v[..., None] * k[:, :, :, None, :]
    return S, jnp.einsum('pbhvk,pbhk->pbhv', S, r)


def rwkv7_mixer(proj, mu, w0, w2, a0, a2, g2, k_k, k_a, r_k, ln_g, ln_b):
    xf = proj.astype(jnp.float32)
    xs = xf + mu * (centred_shift(xf) - xf)
    r, k, v, wl, al, gl = jnp.split(xs, RW_SPLITS, axis=-1)
    w = -jax.nn.softplus(-(w0[:, None, None] + jnp.einsum('btr,prc->pbtc', jnp.tanh(wl), w2))) - 0.5
    decay = jnp.exp(-jnp.exp(w))
    a = jax.nn.sigmoid(a0[:, None, None] + jnp.einsum('btr,prc->pbtc', al, a2))
    g = jax.nn.sigmoid(gl) @ g2
    kk = heads(k * k_k, RW_HEADS)
    kk = (kk / jnp.maximum(jnp.linalg.norm(kk, axis=-1, keepdims=True), 1e-12)).reshape(k.shape)
    k_dir = k * (1.0 + (a - 1.0) * k_a)

    def time_major(s):
        return jnp.moveaxis(heads(s, RW_HEADS), 2, 0)

    xs_scan = (time_major(bidir(r, r)), time_major(bidir(decay[0], decay[1])),
               time_major(bidir(k_dir[0], k_dir[1])), time_major(bidir(v, v)),
               time_major(bidir(-kk, -kk)), time_major(bidir(kk * a[0], kk * a[1])))
    Bsz = proj.shape[0]
    S0 = jnp.zeros((2, Bsz, RW_HEADS, RW_HEAD_DIM, RW_HEAD_DIM), jnp.float32)
    _, o = lax.scan(rwkv7_step, S0, xs_scan)
    o = merge_dirs(jnp.moveaxis(o, 0, 2))
    mean = jnp.mean(o, axis=-1, keepdims=True)
    var = jnp.mean(jnp.square(o - mean), axis=-1, keepdims=True)
    o = (o - mean) * lax.rsqrt(var + RW_LN_EPS) * heads(ln_g, RW_HEADS) + heads(ln_b, RW_HEADS)
    bonus = jnp.sum(jnp.sum(heads(r, RW_HEADS)[None] * heads(k_dir, RW_HEADS) * heads(r_k, RW_HEADS),
                            axis=-1, keepdims=True), axis=0) * heads(v, RW_HEADS)
    return (o + bonus).reshape(proj.shape[:-1] + (RW_WIDTH,)) * g


def moe_ffn(h, router_w, router_b, w_gu, b_gu, w_dn, b_dn):
    n, d = h.shape
    logits = (h @ router_w + router_b).astype(jnp.float32)
    top_vals, top_idx = lax.top_k(logits, TOP_K)
    gates = jax.nn.softmax(top_vals, axis=-1)
    n_assign = n * TOP_K
    flat_e = top_idx.reshape(-1).astype(jnp.int32)
    flat_tok = jnp.arange(n_assign, dtype=jnp.int32) // TOP_K
    order = jnp.argsort(flat_e)
    sorted_e = flat_e[order]
    counts = jnp.bincount(flat_e, length=N_EXPERTS).astype(jnp.int32)
    padded = (counts + MOE_BLOCK - 1) // MOE_BLOCK * MOE_BLOCK
    grp_start = jnp.cumsum(counts) - counts
    pad_end = jnp.cumsum(padded)
    pad_start = pad_end - padded
    dest = pad_start[sorted_e] + jnp.arange(n_assign, dtype=jnp.int32) - grp_start[sorted_e]
    cap = n_assign + N_EXPERTS * MOE_BLOCK
    n_blocks = cap // MOE_BLOCK
    row_tok = jnp.full((cap,), n, dtype=jnp.int32).at[dest].set(flat_tok[order])
    row_gate = jnp.zeros((cap,), jnp.float32).at[dest].set(gates.reshape(-1)[order])
    blk_expert = jnp.minimum(jnp.searchsorted(pad_end, jnp.arange(n_blocks, dtype=jnp.int32) * MOE_BLOCK,
                                              side='right'), N_EXPERTS - 1)
    h_pad = jnp.concatenate([h, jnp.zeros((1, d), h.dtype)], axis=0)

    def expert_block(args):
        tok, gate, e = args
        gu = h_pad[tok] @ w_gu[e] + b_gu[e]
        glu = jnp.minimum(gu[:, :D_FF], SWIGLU_LIMIT)
        lin = jnp.clip(gu[:, D_FF:], -SWIGLU_LIMIT, SWIGLU_LIMIT)
        act = glu * jax.nn.sigmoid(SWIGLU_ALPHA * glu) * (lin + 1.0)
        return (act @ w_dn[e] + b_dn[e]) * gate[:, None]

    yb = lax.map(expert_block, (row_tok.reshape(n_blocks, MOE_BLOCK),
                                row_gate.reshape(n_blocks, MOE_BLOCK), blk_expert))
    y = jnp.zeros((n + 1, d), yb.dtype).at[row_tok].add(yb.reshape(cap, d))
    return y[:n].astype(h.dtype)


def encoder_trunk(x, norm_mix_g, w_in, hg_lower_bound, hg_norm_g, rw_mu, rw_w0, rw_w2, rw_a0, rw_a2,
                  rw_g2, rw_k_k, rw_k_a, rw_r_k, rw_ln_g, rw_ln_b, w_out, norm_ffn_g, router_w, router_b,
                  exp_w_gate_up, exp_b_gate_up, exp_w_down, exp_b_down, final_norm_g):
    lbs = jnp.cumsum(jax.nn.softmax(hg_lower_bound.astype(jnp.float32), axis=0), axis=0)
    for l in range(DEPTH):
        h = rmsnorm(x, norm_mix_g[l])
        proj = h @ w_in[l]
        y_hg = hgrn2_mixer(proj[..., :HG_COLS], lbs[l], hg_norm_g[l])
        y_rw = rwkv7_mixer(proj[..., HG_COLS:], rw_mu[l], rw_w0[l], rw_w2[l], rw_a0[l], rw_a2[l], rw_g2[l],
                           rw_k_k[l], rw_k_a[l], rw_r_k[l], rw_ln_g[l], rw_ln_b[l])
        x = x + jnp.concatenate([y_hg, y_rw], axis=-1).astype(x.dtype) @ w_out[l]
        h = rmsnorm(x, norm_ffn_g[l])
        y = moe_ffn(h.reshape(-1, D_MODEL), router_w[l], router_b[l], exp_w_gate_up[l], exp_b_gate_up[l],
                    exp_w_down[l], exp_b_down[l])
        x = x + y.reshape(x.shape)
    return rmsnorm(x, final_norm_g)


def setup_inputs(seed: int = 0) -> dict:
    key = jax.random.key(seed)
    ks = jax.random.split(key, 26)
    f32 = jnp.float32

    def nrm(k, shape, s):
        return jax.random.normal(k, shape, f32) * s

    L, E, F = DEPTH, N_EXPERTS, D_FF
    return {
        "x_prompt": nrm(ks[0], (BATCH, SEQ, D_MODEL), 1.0),
        "x_sample": nrm(ks[1], (DEC_BATCH, DEC_SEQ, D_MODEL), 1.0),
        "norm_mix_g": 1.0 + nrm(ks[2], (L, D_MODEL), 0.02),
        "w_in": nrm(ks[3], (L, D_MODEL, IN_COLS), D_MODEL ** -0.5),
        "hg_lower_bound": nrm(ks[4], (L + 1, HG_WIDTH), 0.5),
        "hg_norm_g": 1.0 + nrm(ks[5], (L, HG_WIDTH), 0.02),
        "rw_mu": jax.random.uniform(ks[6], (L, RW_COLS), f32, 0.2, 0.8),
        "rw_w0": jax.random.uniform(ks[7], (L, 2, RW_WIDTH), f32, -6.0, -1.0),
        "rw_w2": nrm(ks[8], (L, 2, RW_DECAY_RANK, RW_WIDTH), 0.5 * RW_DECAY_RANK ** -0.5),
        "rw_a0": nrm(ks[9], (L, 2, RW_WIDTH), 0.1),
        "rw_a2": nrm(ks[10], (L, 2, RW_AAA_RANK, RW_WIDTH), RW_AAA_RANK ** -0.5),
        "rw_g2": nrm(ks[11], (L, RW_GATE_RANK, RW_WIDTH), RW_GATE_RANK ** -0.5),
        "rw_k_k": 0.85 + nrm(ks[12], (L, RW_WIDTH), 0.05),
        "rw_k_a": 1.0 + nrm(ks[13], (L, RW_WIDTH), 0.05),
        "rw_r_k": nrm(ks[14], (L, RW_WIDTH), 0.1),
        "rw_ln_g": 1.0 + nrm(ks[15], (L, RW_WIDTH), 0.02),
        "rw_ln_b": nrm(ks[16], (L, RW_WIDTH), 0.02),
        "w_out": nrm(ks[17], (L, D_MODEL, D_MODEL), D_MODEL ** -0.5),
        "norm_ffn_g": 1.0 + nrm(ks[18], (L, D_MODEL), 0.02),
        "router_w": nrm(ks[19], (L, D_MODEL, E), D_MODEL ** -0.5),
        "router_b": nrm(ks[20], (L, E), 0.01),
        "exp_w_gate_up": nrm(ks[21], (L, E, D_MODEL, 2 * F), D_MODEL ** -0.5),
        "exp_b_gate_up": nrm(ks[22], (L, E, 2 * F), 0.01),
        "exp_w_down": nrm(ks[23], (L, E, F, D_MODEL), F ** -0.5),
        "exp_b_down": nrm(ks[24], (L, E, D_MODEL), 0.01),
        "final_norm_g": 1.0 + nrm(ks[25], (D_MODEL,), 0.02),
    }


def reference(x_prompt, x_sample, norm_mix_g, w_in, hg_lower_bound, hg_norm_g, rw_mu, rw_w0, rw_w2, rw_a0,
              rw_a2, rw_g2, rw_k_k, rw_k_a, rw_r_k, rw_ln_g, rw_ln_b, w_out, norm_ffn_g, router_w, router_b,
              exp_w_gate_up, exp_b_gate_up, exp_w_down, exp_b_down, final_norm_g):
    y_prompt = encoder_trunk(x_prompt, norm_mix_g, w_in, hg_lower_bound, hg_norm_g, rw_mu, rw_w0, rw_w2, rw_a0,
                             rw_a2, rw_g2, rw_k_k, rw_k_a, rw_r_k, rw_ln_g, rw_ln_b, w_out, norm_ffn_g, router_w,
                             router_b, exp_w_gate_up, exp_b_gate_up, exp_w_down, exp_b_down, final_norm_g)
    y_sample = encoder_trunk(x_sample, norm_mix_g, w_in, hg_lower_bound, hg_norm_g, rw_mu, rw_w0, rw_w2, rw_a0,
                             rw_a2, rw_g2, rw_k_k, rw_k_a, rw_r_k, rw_ln_g, rw_ln_b, w_out, norm_ffn_g, router_w,
                             router_b, exp_w_gate_up, exp_b_gate_up, exp_w_down, exp_b_down, final_norm_g)
    return (y_prompt, y_sample)
```

```python
import functools

import jax
import jax.numpy as jnp
from jax import lax
from jax.experimental import pallas as pl
from jax.experimental.pallas import tpu as pltpu

F32 = jnp.float32
BF16 = jnp.bfloat16

D_MODEL = 1024
HG_WIDTH = 512
HG_HEAD_DIM = 128
HG_HEADS = HG_WIDTH // HG_HEAD_DIM
RW_WIDTH = 512
RW_HEAD_DIM = 64
RW_DECAY_RANK = 64
RW_AAA_RANK = 64
RW_GATE_RANK = 128
HG_COLS = 5 * HG_WIDTH
RW_COLS = 3 * RW_WIDTH + RW_DECAY_RANK + RW_AAA_RANK + RW_GATE_RANK
RW_TAIL = RW_COLS - 3 * RW_WIDTH
N_EXPERTS = 32
TOP_K = 4
D_FF = D_MODEL
SWIGLU_LIMIT = 7.0
SWIGLU_ALPHA = 1.702
NORM_EPS = 1e-5
RW_LN_EPS = 64e-5

V7X_VMEM_LIMIT_BYTES = 56 * 1024 * 1024
SUBLANES = 8

PROJ_TILE = 256
HG_CHUNK = 64
HG_BLOCK = 256
RW_CHUNK = 64
RW_GROUP = 256
RW_BLOCK = 128
MOE_BLOCK = 256
COMBINE_TILE = 128
EXP_CLAMP = 80.0


def _dot(a, b):
    return jnp.dot(a, b, preferred_element_type=F32)


def _dot_nt(a, b):
    return lax.dot_general(a, b, (((1,), (1,)), ((), ())), preferred_element_type=F32)


def _dot_tn(a, b):
    return lax.dot_general(a, b, (((0,), (0,)), ((), ())), preferred_element_type=F32)


def _split(x):
    hi = x.astype(BF16)
    lo = (x - hi.astype(F32)).astype(BF16)
    return hi, lo


def _dot_exact_lhs(a_bf16, x):
    hi, lo = _split(x)
    return _dot(a_bf16, hi) + _dot(a_bf16, lo)


def _dot3(x, w_hi, w_lo):
    hi, lo = _split(x)
    return _dot(hi, w_hi) + _dot(lo, w_hi) + _dot(hi, w_lo)


def _sigmoid(x):
    return 1.0 / (1.0 + jnp.exp(-x))


def _is_seq_boundary(segs, row):
    base = 0
    hit = row == sum(n * t for n, t in segs)
    for n, t in segs:
        inside = jnp.logical_and(row >= base, row < base + n * t)
        hit = jnp.logical_or(hit, jnp.logical_and(inside, lax.rem(row - base, t) == 0))
        base += n * t
    return hit


def _params(vmem=V7X_VMEM_LIMIT_BYTES):
    return pltpu.CompilerParams(dimension_semantics=("arbitrary",), vmem_limit_bytes=vmem)


def _in_proj_body(segs, xp_ref, x_ref, xn_ref, g_ref, whg_ref, wrw_ref, mu_ref, hg_ref, rw_ref):
    tm = x_ref.shape[0]
    row0 = pl.program_id(0) * tm
    at_start = _is_seq_boundary(segs, row0)
    at_end = _is_seq_boundary(segs, row0 + tm)
    g = g_ref[...]

    def nrm(x):
        return x * lax.rsqrt(jnp.mean(x * x, axis=-1, keepdims=True) + NORM_EPS) * g

    h = nrm(x_ref[...])
    hp = jnp.where(at_start, 0.0, nrm(xp_ref[...]))
    hn = jnp.where(at_end, 0.0, nrm(xn_ref[...]))
    hb = h.astype(BF16)
    for c in range(HG_COLS // HG_WIDTH):
        cs = slice(c * HG_WIDTH, (c + 1) * HG_WIDTH)
        hg_ref[:, cs] = _dot(hb, whg_ref[:, cs]).astype(BF16)
    hs = jnp.concatenate([hp.astype(BF16), hb, hn.astype(BF16)], axis=0)
    p = _dot(hs, wrw_ref[...])
    rows = tm + 2 * SUBLANES
    cur = p[SUBLANES:SUBLANES + tm]
    prev = pltpu.roll(p, 1, 0)[SUBLANES:SUBLANES + tm]
    nxt = pltpu.roll(p, rows - 1, 0)[SUBLANES:SUBLANES + tm]
    mu = mu_ref[...]
    rw_ref[...] = (cur + mu * (0.5 * (prev + nxt) - cur)).astype(BF16)


def _in_proj(x, norm_g, w_hg, w_rw, mu, segs):
    n, d = x.shape
    tm = PROJ_TILE
    hb = tm // SUBLANES
    last = n // SUBLANES - 1
    return pl.pallas_call(
        functools.partial(_in_proj_body, segs),
        grid=(n // tm,),
        in_specs=[
            pl.BlockSpec((SUBLANES, d), lambda i: (jnp.maximum(i * hb - 1, 0), 0)),
            pl.BlockSpec((tm, d), lambda i: (i, 0)),
            pl.BlockSpec((SUBLANES, d), lambda i: (jnp.minimum((i + 1) * hb, last), 0)),
            pl.BlockSpec((1, d), lambda i: (0, 0)),
            pl.BlockSpec((d, HG_COLS), lambda i: (0, 0)),
            pl.BlockSpec((d, RW_COLS), lambda i: (0, 0)),
            pl.BlockSpec((1, RW_COLS), lambda i: (0, 0)),
        ],
        out_specs=[
            pl.BlockSpec((tm, HG_COLS), lambda i: (i, 0)),
            pl.BlockSpec((tm, RW_COLS), lambda i: (i, 0)),
        ],
        out_shape=[jax.ShapeDtypeStruct((n, HG_COLS), BF16), jax.ShapeDtypeStruct((n, RW_COLS), BF16)],
        compiler_params=_params(),
        name="in_proj",
    )(x, x, x, norm_g, w_hg, w_rw, mu)


def _tri(c, reverse):
    t = lax.broadcasted_iota(jnp.int32, (c, c), 0)
    s = lax.broadcasted_iota(jnp.int32, (c, c), 1)
    return (s >= t) if reverse else (s <= t)


def _hgrn2_body(segs, reverse, nblk, *refs):
    if reverse:
        q_ref, f_ref, i_ref, lb_ref, out_ref, st_ref = refs
    else:
        q_ref, f_ref, i_ref, g_ref, ob_ref, lb_ref, ng_ref, out_ref, st_ref = refs
    tb = q_ref.shape[0]
    c = HG_CHUNK
    step = pl.program_id(0)
    blk = (nblk - 1 - step) if reverse else step
    edge = (blk + 1) * tb if reverse else blk * tb

    @pl.when(_is_seq_boundary(segs, edge))
    def _():
        st_ref[...] = jnp.zeros_like(st_ref)

    lbp = lb_ref[...]
    e0 = jnp.exp(lbp[0:1] - jnp.maximum(lbp[0:1], lbp[1:2]))
    e1 = jnp.exp(lbp[1:2] - jnp.maximum(lbp[0:1], lbp[1:2]))
    lb = e0 / (e0 + e1)
    tri = _tri(c, reverse)
    tri_b = tri.astype(BF16)
    mid = c // 2 if reverse else c // 2 - 1
    last = 0 if reverse else c - 1
    nchunk = tb // c
    order = range(nchunk - 1, -1, -1) if reverse else range(nchunk)
    for ci in order:
        rs = slice(ci * c, (ci + 1) * c)
        q = q_ref[rs, :].astype(F32)
        f = lb + (1.0 - lb) * _sigmoid(f_ref[rs, :].astype(F32))
        kx = 1.0 - f
        cum = _dot_exact_lhs(tri_b, jnp.log(f))
        rho = cum[mid:mid + 1]
        tot = cum[last:last + 1]
        qs = q * _sigmoid(q)
        qh = (qs * jnp.exp(jnp.minimum(cum - rho, EXP_CLAMP))).astype(BF16)
        kh = (kx * jnp.exp(jnp.minimum(rho - cum, EXP_CLAMP))).astype(BF16)
        qt = (qs * jnp.exp(cum)).astype(BF16)
        ke = (kx * jnp.exp(tot - cum)).astype(BF16)
        vv = i_ref[rs, :]
        etot = jnp.exp(tot)
        outs = []
        for h in range(HG_HEADS):
            hs = slice(h * HG_HEAD_DIM, (h + 1) * HG_HEAD_DIM)
            st = st_ref[h]
            sc = jnp.where(tri, _dot_nt(qh[:, hs], kh[:, hs]), 0.0)
            o = _dot(sc.astype(BF16), vv[:, hs]) + _dot_nt(qt[:, hs], st.astype(BF16))
            st_ref[h] = st * etot[:, hs] + _dot_tn(vv[:, hs], ke[:, hs])
            outs.append(o)
        if reverse:
            for h in range(HG_HEADS):
                out_ref[rs, h * HG_HEAD_DIM:(h + 1) * HG_HEAD_DIM] = outs[h].astype(out_ref.dtype)
        else:
            ng = ng_ref[...]
            gg = g_ref[rs, :].astype(F32)
            gate = gg * _sigmoid(gg)
            for h in range(HG_HEADS):
                hs = slice(h * HG_HEAD_DIM, (h + 1) * HG_HEAD_DIM)
                o = outs[h] + ob_ref[rs, hs].astype(F32)
                o = o * lax.rsqrt(jnp.mean(o * o, axis=-1, keepdims=True) + NORM_EPS) * ng[:, hs]
                out_ref[rs, hs] = (o * gate[:, hs]).astype(out_ref.dtype)


def _hgrn2(hg, lower_bound, norm_g, segs):
    n = hg.shape[0]
    tb = HG_BLOCK
    nblk = n // tb
    w = HG_WIDTH
    scratch = [pltpu.VMEM((HG_HEADS, HG_HEAD_DIM, HG_HEAD_DIM), F32)]

    def col(j, rev):
        if rev:
            return pl.BlockSpec((tb, w), lambda i: (nblk - 1 - i, j))
        return pl.BlockSpec((tb, w), lambda i: (i, j))

    o_b = pl.pallas_call(
        functools.partial(_hgrn2_body, segs, True, nblk),
        grid=(nblk,),
        in_specs=[col(0, True), col(2, True), col(3, True), pl.BlockSpec((2, w), lambda i: (0, 0))],
        out_specs=col(0, True),
        out_shape=jax.ShapeDtypeStruct((n, w), BF16),
        scratch_shapes=scratch,
        compiler_params=_params(),
        name="hgrn2_bwd",
    )(hg, hg, hg, lower_bound)
    return pl.pallas_call(
        functools.partial(_hgrn2_body, segs, False, nblk),
        grid=(nblk,),
        in_specs=[col(0, False), col(1, False), col(3, False), col(4, False), col(0, False),
                  pl.BlockSpec((2, w), lambda i: (0, 0)), pl.BlockSpec((1, w), lambda i: (0, 0))],
        out_specs=col(0, False),
        out_shape=jax.ShapeDtypeStruct((n, w), BF16),
        scratch_shapes=scratch,
        compiler_params=_params(),
        name="hgrn2_fwd",
    )(hg, hg, hg, hg, o_b, lower_bound, norm_g)


def _block_diag_mask(n, blk):
    shift = blk.bit_length() - 1
    assert 1 << shift == blk
    r = lax.broadcasted_iota(jnp.int32, (n, n), 0) >> shift
    c = lax.broadcasted_iota(jnp.int32, (n, n), 1) >> shift
    return r == c


def _rwkv7_body(segs, reverse, nblk, *refs):
    if reverse:
        (r_ref, k_ref, v_ref, t_ref, w0_ref, w2h_ref, w2l_ref, a0_ref, a2_ref, kk_ref, ka_ref,
         out_ref, s_ref) = refs
    else:
        (r_ref, k_ref, v_ref, t_ref, ob_ref, w0_ref, w2h_ref, w2l_ref, a0_ref, a2_ref, a0b_ref, a2b_ref,
         g2_ref, kk_ref, ka_ref, rk_ref, lng_ref, lnb_ref, out_ref, s_ref) = refs
    tb = r_ref.shape[0]
    c = RW_CHUNK
    gw = RW_GROUP
    ngroup = RW_WIDTH // gw
    hpg = gw // RW_HEAD_DIM
    step = pl.program_id(0)
    blk = (nblk - 1 - step) if reverse else step
    edge = (blk + 1) * tb if reverse else blk * tb

    @pl.when(_is_seq_boundary(segs, edge))
    def _():
        s_ref[...] = jnp.zeros_like(s_ref)

    tri = _tri(c, reverse)
    tri_b = tri.astype(BF16)
    bd = _block_diag_mask(gw, RW_HEAD_DIM)
    bd_b = bd.astype(BF16)
    head_ones = _block_diag_mask(RW_WIDTH, RW_HEAD_DIM).astype(BF16)
    tt = lax.broadcasted_iota(jnp.int32, (c, gw), 0)
    ss = lax.broadcasted_iota(jnp.int32, (c, gw), 1) & (c - 1)
    m_strict = (ss > tt) if reverse else (ss < tt)
    m_incl = (ss >= tt) if reverse else (ss <= tt)
    eye = (ss == tt).astype(F32)
    last = 0 if reverse else c - 1

    def blockdiag(x):
        return jnp.concatenate([x.astype(BF16)] * hpg, axis=0) * bd_b

    def headsum(x):
        return _dot_exact_lhs_t(x, head_ones)

    w0 = w0_ref[...]
    a0 = a0_ref[...]
    k_k = kk_ref[...]
    k_a = ka_ref[...]
    nchunk = tb // c
    order = range(nchunk - 1, -1, -1) if reverse else range(nchunk)
    for ci in order:
        rs = slice(ci * c, (ci + 1) * c)
        r = r_ref[rs, :].astype(F32)
        k = k_ref[rs, :].astype(F32)
        vb = v_ref[rs, :]
        lat = t_ref[rs, 0:2 * RW_DECAY_RANK].astype(F32)
        wpre = w0 + _dot3(jnp.tanh(lat), w2h_ref[...], w2l_ref[...])
        sp = jnp.maximum(-wpre, 0.0) + jnp.log(1.0 + jnp.exp(-jnp.abs(wpre)))
        lw = -jnp.exp(-sp - 0.5)
        a_lr = _sigmoid(a0 + _dot(lat.astype(BF16), a2_ref[...]))
        kkr = k * k_k
        kkn = kkr / jnp.maximum(jnp.sqrt(headsum(kkr * kkr)), 1e-12)
        kd = k * (1.0 + (a_lr - 1.0) * k_a)
        cum = _dot_exact_lhs(tri_b, lw)
        ex = cum - lw
        tot = cum[last:last + 1]
        ecn = jnp.exp(-cum)
        rt = r * jnp.exp(cum)
        at = -kkn * jnp.exp(ex)
        bvec = kkn * a_lr
        bt = bvec * ecn
        kt = kd * ecn
        eend = jnp.exp(tot - cum)
        bh = bvec * eend
        kh = kd * eend
        etot = jnp.exp(tot)
        o_groups = []
        for g in range(ngroup):
            gs = slice(g * gw, (g + 1) * gw)
            ar = jnp.concatenate([at[:, gs], rt[:, gs]], axis=0).astype(BF16)
            g1 = _dot_nt(ar, blockdiag(bt[:, gs]))
            g2 = _dot_nt(ar, blockdiag(kt[:, gs]))
            n_ab = jnp.where(m_strict, g1[:c], 0.0)
            n_ak = jnp.where(m_strict, g2[:c], 0.0)
            m_rb = jnp.where(m_incl, g1[c:], 0.0)
            m_rk = jnp.where(m_incl, g2[c:], 0.0)
            tinv = eye + n_ab
            pw = n_ab
            nsq = c.bit_length() - 1
            for j in range(1, nsq):
                pw = _dot(pw.astype(BF16), blockdiag(pw))
                tinv = tinv + _dot(pw.astype(BF16), blockdiag(tinv))
            s_g = s_ref[g]
            ars = _dot_nt(ar, s_g.astype(BF16))
            vbd = blockdiag(vb[:, gs])
            rhs = ars[:c] + _dot(n_ak.astype(BF16), vbd)
            u = _dot(tinv.astype(BF16), blockdiag(rhs))
            o = ars[c:] + _dot(m_rb.astype(BF16), blockdiag(u)) + _dot(m_rk.astype(BF16), vbd)
            uv = jnp.concatenate([u.astype(BF16), vb[:, gs]], axis=0)
            bk = jnp.concatenate([bh[:, gs], kh[:, gs]], axis=0).astype(BF16)
            s_ref[g] = s_g * etot[:, gs] + jnp.where(bd, _dot_tn(uv, bk), 0.0)
            o_groups.append(o)
        if reverse:
            for g in range(ngroup):
                out_ref[rs, g * gw:(g + 1) * gw] = o_groups[g].astype(out_ref.dtype)
        else:
            o = jnp.concatenate(o_groups, axis=1) + ob_ref[rs, :].astype(F32)
            inv_n = 1.0 / RW_HEAD_DIM
            mean = headsum(o) * inv_n
            d = o - mean
            var = headsum(d * d) * inv_n
            o = d * lax.rsqrt(var + RW_LN_EPS) * lng_ref[...] + lnb_ref[...]
            a_lr_b = _sigmoid(a0b_ref[...] + _dot(lat.astype(BF16), a2b_ref[...]))
            kd_b = k * (1.0 + (a_lr_b - 1.0) * k_a)
            bonus = headsum(r * (kd + kd_b) * rk_ref[...]) * vb.astype(F32)
            gate = _dot(_sigmoid(t_ref[rs, 2 * RW_DECAY_RANK:].astype(F32)).astype(BF16), g2_ref[...])
            out_ref[rs, :] = ((o + bonus) * gate).astype(out_ref.dtype)


def _dot_exact_lhs_t(x, ones_bf16):
    hi, lo = _split(x)
    return _dot(hi, ones_bf16) + _dot(lo, ones_bf16)


def _rwkv7(rw, p, segs):
    n = rw.shape[0]
    tb = RW_BLOCK
    nblk = n // tb
    w = RW_WIDTH
    scratch = [pltpu.VMEM((w // RW_GROUP, RW_GROUP, RW_GROUP), F32)]

    def rows(j, width, rev):
        if rev:
            return pl.BlockSpec((tb, width), lambda i: (nblk - 1 - i, j))
        return pl.BlockSpec((tb, width), lambda i: (i, j))

    def full(a):
        return pl.BlockSpec(a.shape, lambda i: (0,) * a.ndim)

    tail_blk = (3 * w) // RW_TAIL
    bwd_params = [p["w0_b"], p["w2h_b"], p["w2l_b"], p["a0_b"], p["a2_b"], p["k_k"], p["k_a"]]
    o_b = pl.pallas_call(
        functools.partial(_rwkv7_body, segs, True, nblk),
        grid=(nblk,),
        in_specs=[rows(0, w, True), rows(1, w, True), rows(2, w, True), rows(tail_blk, RW_TAIL, True)]
        + [full(a) for a in bwd_params],
        out_specs=rows(0, w, True),
        out_shape=jax.ShapeDtypeStruct((n, w), BF16),
        scratch_shapes=scratch,
        compiler_params=_params(),
        name="rwkv7_bwd",
    )(rw, rw, rw, rw, *bwd_params)
    fwd_params = [p["w0_f"], p["w2h_f"], p["w2l_f"], p["a0_f"], p["a2_f"], p["a0_b"], p["a2_b"], p["g2"],
                  p["k_k"], p["k_a"], p["r_k"], p["ln_g"], p["ln_b"]]
    return pl.pallas_call(
        functools.partial(_rwkv7_body, segs, False, nblk),
        grid=(nblk,),
        in_specs=[rows(0, w, False), rows(1, w, False), rows(2, w, False), rows(tail_blk, RW_TAIL, False),
                  rows(0, w, False)] + [full(a) for a in fwd_params],
        out_specs=rows(0, w, False),
        out_shape=jax.ShapeDtypeStruct((n, w), BF16),
        scratch_shapes=scratch,
        compiler_params=_params(),
        name="rwkv7_fwd",
    )(rw, rw, rw, rw, o_b, *fwd_params)


def _out_router_body(x_ref, yh_ref, yr_ref, woa_ref, wob_ref, g_ref, rwh_ref, rwl_ref, rb_ref,
                     x2_ref, h2_ref, idx_ref, gate_ref):
    x2 = x_ref[...] + _dot(yh_ref[...], woa_ref[...]) + _dot(yr_ref[...], wob_ref[...])
    x2_ref[...] = x2
    h2 = x2 * lax.rsqrt(jnp.mean(x2 * x2, axis=-1, keepdims=True) + NORM_EPS) * g_ref[...]
    h2_ref[...] = h2
    hi, lo = _split(h2)
    logits = (_dot_nt(rwh_ref[...], hi) + _dot_nt(rwh_ref[...], lo) + _dot_nt(rwl_ref[...], hi)) + rb_ref[...]
    ids = lax.broadcasted_iota(jnp.int32, logits.shape, 0)
    vals = logits
    tops, picks = [], []
    for _ in range(TOP_K):
        m = jnp.max(vals, axis=0, keepdims=True)
        pick = jnp.min(jnp.where(vals == m, ids, N_EXPERTS), axis=0, keepdims=True)
        vals = jnp.where(ids == pick, -jnp.inf, vals)
        tops.append(m)
        picks.append(pick)
    es = [jnp.exp(t - tops[0]) for t in tops]
    den = es[0] + es[1] + es[2] + es[3]
    idx_ref[...] = jnp.concatenate(picks, axis=0)
    gate_ref[...] = jnp.concatenate([e / den for e in es], axis=0)


def _out_router(x, y_hg, y_rw, wo_a, wo_b, norm_g, rw_hi, rw_lo, rb):
    n, d = x.shape
    tm = PROJ_TILE

    def full(a):
        return pl.BlockSpec(a.shape, lambda i: (0,) * a.ndim)

    return pl.pallas_call(
        _out_router_body,
        grid=(n // tm,),
        in_specs=[pl.BlockSpec((tm, d), lambda i: (i, 0)),
                  pl.BlockSpec((tm, HG_WIDTH), lambda i: (i, 0)),
                  pl.BlockSpec((tm, RW_WIDTH), lambda i: (i, 0)),
                  full(wo_a), full(wo_b), full(norm_g), full(rw_hi), full(rw_lo), full(rb)],
        out_specs=[pl.BlockSpec((tm, d), lambda i: (i, 0)),
                   pl.BlockSpec((tm, d), lambda i: (i, 0)),
                   pl.BlockSpec((TOP_K, tm), lambda i: (0, i)),
                   pl.BlockSpec((TOP_K, tm), lambda i: (0, i))],
        out_shape=[jax.ShapeDtypeStruct((n, d), F32), jax.ShapeDtypeStruct((n, d), F32),
                   jax.ShapeDtypeStruct((TOP_K, n), jnp.int32), jax.ShapeDtypeStruct((TOP_K, n), F32)],
        compiler_params=_params(),
        name="out_router",
    )(x, y_hg, y_rw, wo_a, wo_b, norm_g, rw_hi, rw_lo, rb)


def _row_copy(src_hbm, row, dst_vmem, slot, sem):
    return pltpu.make_async_copy(src_hbm.at[pl.ds(row, 1)], dst_vmem.at[pl.ds(slot, 1)], sem)


def _moe_body(be_ref, nu_ref, tok_ref, gate_ref, h_hbm, wgu_ref, bgu_ref, wdn_ref, bdn_ref, out_ref,
              xbuf, sem):
    b = pl.program_id(0)
    blk = xbuf.shape[0]

    @pl.when(b < nu_ref[0])
    def _():
        for r in range(blk):
            _row_copy(h_hbm, tok_ref[0, 0, r], xbuf, r, sem).start()
        for r in range(blk):
            _row_copy(h_hbm, 0, xbuf, r, sem).wait()
        x = xbuf[...].astype(BF16)
        gu = _dot(x, wgu_ref[0]) + bgu_ref[0]
        glu = jnp.minimum(gu[:, :D_FF], SWIGLU_LIMIT)
        lin = jnp.clip(gu[:, D_FF:], -SWIGLU_LIMIT, SWIGLU_LIMIT)
        act = glu * _sigmoid(SWIGLU_ALPHA * glu) * (lin + 1.0)
        y = _dot(act.astype(BF16), wdn_ref[0]) + bdn_ref[0]
        rr = lax.broadcasted_iota(jnp.int32, (blk, blk), 0)
        cc = lax.broadcasted_iota(jnp.int32, (blk, blk), 1)
        gcol = jnp.sum(jnp.where(rr == cc, gate_ref[0], 0.0), axis=1, keepdims=True)
        out_ref[...] = (y * gcol).astype(out_ref.dtype)

    @pl.when(b >= nu_ref[0])
    def _():
        out_ref[...] = jnp.zeros_like(out_ref)


def _moe(h2, blk_expert, n_used, row_tok, row_gate, w_gu, b_gu, w_dn, b_dn):
    d = h2.shape[1]
    blk = MOE_BLOCK
    nblk = row_tok.shape[0]
    grid_spec = pltpu.PrefetchScalarGridSpec(
        num_scalar_prefetch=2,
        grid=(nblk,),
        in_specs=[
            pl.BlockSpec((1, 1, blk), lambda b, be, nu: (b, 0, 0), memory_space=pltpu.SMEM),
            pl.BlockSpec((1, 1, blk), lambda b, be, nu: (b, 0, 0)),
            pl.BlockSpec(memory_space=pl.ANY),
            pl.BlockSpec((1, d, 2 * D_FF), lambda b, be, nu: (be[b], 0, 0)),
            pl.BlockSpec((1, 1, 2 * D_FF), lambda b, be, nu: (be[b], 0, 0)),
            pl.BlockSpec((1, D_FF, d), lambda b, be, nu: (be[b], 0, 0)),
            pl.BlockSpec((1, 1, d), lambda b, be, nu: (be[b], 0, 0)),
        ],
        out_specs=pl.BlockSpec((blk, d), lambda b, be, nu: (b, 0)),
        scratch_shapes=[pltpu.VMEM((blk, d), F32), pltpu.SemaphoreType.DMA],
    )
    return pl.pallas_call(
        _moe_body,
        grid_spec=grid_spec,
        out_shape=jax.ShapeDtypeStruct((nblk * blk, d), F32),
        compiler_params=_params(),
        name="moe_experts",
    )(blk_expert, n_used, row_tok, row_gate, h2, w_gu, b_gu, w_dn, b_dn)


def _combine_body(pos_ref, x2_ref, g_ref, yb_hbm, out_ref, ybuf, sem):
    tm = x2_ref.shape[0]
    for t in range(tm):
        for k in range(TOP_K):
            pltpu.make_async_copy(yb_hbm.at[pl.ds(pos_ref[0, 0, t * TOP_K + k], 1)],
                                  ybuf.at[k, pl.ds(t, 1)], sem).start()
    for t in range(tm):
        for k in range(TOP_K):
            pltpu.make_async_copy(yb_hbm.at[pl.ds(0, 1)], ybuf.at[k, pl.ds(t, 1)], sem).wait()
    x = x2_ref[...] + ((ybuf[0] + ybuf[1]) + (ybuf[2] + ybuf[3]))
    out_ref[...] = x * lax.rsqrt(jnp.mean(x * x, axis=-1, keepdims=True) + NORM_EPS) * g_ref[...]


def _combine(pos, x2, final_g, yb):
    n, d = x2.shape
    tm = COMBINE_TILE
    return pl.pallas_call(
        _combine_body,
        grid=(n // tm,),
        in_specs=[pl.BlockSpec((1, 1, tm * TOP_K), lambda i: (i, 0, 0), memory_space=pltpu.SMEM),
                  pl.BlockSpec((tm, d), lambda i: (i, 0)),
                  pl.BlockSpec((1, d), lambda i: (0, 0)),
                  pl.BlockSpec(memory_space=pl.ANY)],
        out_specs=pl.BlockSpec((tm, d), lambda i: (i, 0)),
        out_shape=jax.ShapeDtypeStruct((n, d), F32),
        scratch_shapes=[pltpu.VMEM((TOP_K, tm, d), F32), pltpu.SemaphoreType.DMA],
        compiler_params=_params(),
        name="moe_combine",
    )(pos, x2, final_g, yb)


def _routing_tables(idx_t, gate_t):
    n = idx_t.shape[1]
    n_assign = n * TOP_K
    blk = MOE_BLOCK
    flat_e = idx_t.T.reshape(-1)
    flat_g = gate_t.T.reshape(-1)
    order = jnp.argsort(flat_e, stable=True).astype(jnp.int32)
    sorted_e = flat_e[order]
    counts = jnp.zeros((N_EXPERTS,), jnp.int32).at[flat_e].add(1)
    padded = (counts + blk - 1) // blk * blk
    grp_start = jnp.cumsum(counts) - counts
    pad_end = jnp.cumsum(padded)
    pad_start = pad_end - padded
    dest = pad_start[sorted_e] + jnp.arange(n_assign, dtype=jnp.int32) - grp_start[sorted_e]
    nblk = (n_assign + N_EXPERTS * (blk - 1)) // blk + 1
    cap = nblk * blk
    row_tok = jnp.zeros((cap,), jnp.int32).at[dest].set(order // TOP_K)
    row_gate = jnp.zeros((cap,), F32).at[dest].set(flat_g[order])
    pos = jnp.zeros((n_assign,), jnp.int32).at[order].set(dest)
    blk_expert = jnp.minimum(
        jnp.searchsorted(pad_end, jnp.arange(nblk, dtype=jnp.int32) * blk, side="right"),
        N_EXPERTS - 1).astype(jnp.int32)
    n_used = (pad_end[-1] // blk).astype(jnp.int32).reshape(1)
    return (blk_expert, n_used, row_tok.reshape(nblk, 1, blk), row_gate.reshape(nblk, 1, blk),
            pos.reshape(n // COMBINE_TILE, 1, COMBINE_TILE * TOP_K))


def _pad_rows(w, top, total):
    return jnp.zeros((total, w.shape[1]), w.dtype).at[top:top + w.shape[0]].set(w)


def _encoder(x, segs, norm_mix_g, w_in, hg_lower_bound, hg_norm_g, rw_mu, rw_w0, rw_w2, rw_a0, rw_a2, rw_g2,
             rw_k_k, rw_k_a, rw_r_k, rw_ln_g, rw_ln_b, w_out, norm_ffn_g, router_w, router_b, exp_w_gate_up,
             exp_b_gate_up, exp_w_down, exp_b_down, final_norm_g):
    w_in_b = w_in[0].astype(BF16)
    hg, rw = _in_proj(x, norm_mix_g, w_in_b[:, :HG_COLS], w_in_b[:, HG_COLS:], rw_mu, segs)
    y_hg = _hgrn2(hg, hg_lower_bound, hg_norm_g, segs)
    lat = 2 * RW_DECAY_RANK
    p = {"k_k": rw_k_k, "k_a": rw_k_a, "r_k": rw_r_k, "ln_g": rw_ln_g, "ln_b": rw_ln_b,
         "g2": rw_g2[0].astype(BF16)}
    for d, tag in ((0, "f"), (1, "b")):
        w2 = _pad_rows(rw_w2[0, d], 0, lat)
        p["w2h_" + tag], p["w2l_" + tag] = _split(w2)
        p["a2_" + tag] = _pad_rows(rw_a2[0, d], RW_DECAY_RANK, lat).astype(BF16)
        p["w0_" + tag] = rw_w0[0, d][None]
        p["a0_" + tag] = rw_a0[0, d][None]
    y_rw = _rwkv7(rw, p, segs)
    wo = w_out[0].astype(BF16)
    rw_hi, rw_lo = _split(router_w[0].T)
    x2, h2, idx_t, gate_t = _out_router(x, y_hg, y_rw, wo[:HG_WIDTH], wo[HG_WIDTH:], norm_ffn_g, rw_hi, rw_lo,
                                        router_b[0][:, None])
    blk_expert, n_used, row_tok, row_gate, pos = _routing_tables(idx_t, gate_t)
    yb = _moe(h2, blk_expert, n_used, row_tok, row_gate, exp_w_gate_up[0].astype(BF16),
              exp_b_gate_up[0][:, None, :], exp_w_down[0].astype(BF16), exp_b_down[0][:, None, :])
    return _combine(pos, x2, final_norm_g[None], yb)


def kernel(x_prompt, x_sample, norm_mix_g, w_in, hg_lower_bound, hg_norm_g, rw_mu, rw_w0, rw_w2, rw_a0, rw_a2,
           rw_g2, rw_k_k, rw_k_a, rw_r_k, rw_ln_g, rw_ln_b, w_out, norm_ffn_g, router_w, router_b, exp_w_gate_up,
           exp_b_gate_up, exp_w_down, exp_b_down, final_norm_g):
    d = x_prompt.shape[-1]
    segs = (x_prompt.shape[:2], x_sample.shape[:2])
    n_p = x_prompt.shape[0] * x_prompt.shape[1]
    x = jnp.concatenate([x_prompt.reshape(-1, d), x_sample.reshape(-1, d)], axis=0)
    y = _encoder(x, segs, norm_mix_g, w_in, hg_lower_bound, hg_norm_g, rw_mu, rw_w0, rw_w2, rw_a0, rw_a2,
                 rw_g2, rw_k_k, rw_k_a, rw_r_k, rw_ln_g, rw_ln_b, w_out, norm_ffn_g, router_w, router_b,
                 exp_w_gate_up, exp_b_gate_up, exp_w_down, exp_b_down, final_norm_g)
    return y[:n_p].reshape(x_prompt.shape), y[n_p:].reshape(x_sample.shape)
```

```python
import functools

import jax
import jax.numpy as jnp
from jax import lax
from jax.experimental import pallas as pl
from jax.experimental.pallas import tpu as pltpu

F32 = jnp.float32
BF16 = jnp.bfloat16

D_MODEL = 1024
HG_WIDTH = 512
HG_HEAD_DIM = 128
HG_HEADS = HG_WIDTH // HG_HEAD_DIM
RW_WIDTH = 512
RW_HEAD_DIM = 64
RW_DECAY_RANK = 64
RW_AAA_RANK = 64
RW_GATE_RANK = 128
HG_COLS = 5 * HG_WIDTH
RW_COLS = 3 * RW_WIDTH + RW_DECAY_RANK + RW_AAA_RANK + RW_GATE_RANK
RW_TAIL = RW_COLS - 3 * RW_WIDTH
N_EXPERTS = 32
TOP_K = 4
D_FF = D_MODEL
SWIGLU_LIMIT = 7.0
SWIGLU_ALPHA = 1.702
NORM_EPS = 1e-5
RW_LN_EPS = 64e-5

V7X_VMEM_LIMIT_BYTES = 56 * 1024 * 1024
SUBLANES = 8
LANES = 128
ROW_TILE = D_MODEL // LANES
assert ROW_TILE == SUBLANES

PROJ_TILE = 256
HG_CHUNK = 64
HG_BLOCK = 256
RW_CHUNK = 64
RW_GROUP = 256
RW_BLOCK = 128
MOE_BLOCK = 256
COMBINE_TILE = 128
EXP_CLAMP = 80.0


def _dot(a, b):
    return jnp.dot(a, b, preferred_element_type=F32)


def _dot_nt(a, b):
    return lax.dot_general(a, b, (((1,), (1,)), ((), ())), preferred_element_type=F32)


def _dot_tn(a, b):
    return lax.dot_general(a, b, (((0,), (0,)), ((), ())), preferred_element_type=F32)


def _split(x):
    hi = x.astype(BF16)
    lo = (x - hi.astype(F32)).astype(BF16)
    return hi, lo


def _dot_exact_lhs(a_bf16, x):
    hi, lo = _split(x)
    return _dot(a_bf16, hi) + _dot(a_bf16, lo)


def _dot3(x, w_hi, w_lo):
    hi, lo = _split(x)
    return _dot(hi, w_hi) + _dot(lo, w_hi) + _dot(hi, w_lo)


def _sigmoid(x):
    return 1.0 / (1.0 + jnp.exp(-x))


def _is_seq_boundary(segs, row):
    base = 0
    hit = row == sum(n * t for n, t in segs)
    for n, t in segs:
        inside = jnp.logical_and(row >= base, row < base + n * t)
        hit = jnp.logical_or(hit, jnp.logical_and(inside, lax.rem(row - base, t) == 0))
        base += n * t
    return hit


def _params(vmem=V7X_VMEM_LIMIT_BYTES):
    return pltpu.CompilerParams(dimension_semantics=("arbitrary",), vmem_limit_bytes=vmem)


def _in_proj_body(segs, xp_ref, x_ref, xn_ref, g_ref, whg_ref, wrw_ref, mu_ref, hg_ref, rw_ref):
    tm = x_ref.shape[0]
    row0 = pl.program_id(0) * tm
    at_start = _is_seq_boundary(segs, row0)
    at_end = _is_seq_boundary(segs, row0 + tm)
    g = g_ref[...]

    def nrm(x):
        return x * lax.rsqrt(jnp.mean(x * x, axis=-1, keepdims=True) + NORM_EPS) * g

    h = nrm(x_ref[...])
    hp = jnp.where(at_start, 0.0, nrm(xp_ref[...]))
    hn = jnp.where(at_end, 0.0, nrm(xn_ref[...]))
    hb = h.astype(BF16)
    for c in range(HG_COLS // HG_WIDTH):
        cs = slice(c * HG_WIDTH, (c + 1) * HG_WIDTH)
        hg_ref[:, cs] = _dot(hb, whg_ref[:, cs]).astype(BF16)
    hs = jnp.concatenate([hp.astype(BF16), hb, hn.astype(BF16)], axis=0)
    p = _dot(hs, wrw_ref[...])
    rows = tm + 2 * SUBLANES
    cur = p[SUBLANES:SUBLANES + tm]
    prev = pltpu.roll(p, 1, 0)[SUBLANES:SUBLANES + tm]
    nxt = pltpu.roll(p, rows - 1, 0)[SUBLANES:SUBLANES + tm]
    mu = mu_ref[...]
    rw_ref[...] = (cur + mu * (0.5 * (prev + nxt) - cur)).astype(BF16)


def _in_proj(x, norm_g, w_hg, w_rw, mu, segs):
    n, d = x.shape
    tm = PROJ_TILE
    hb = tm // SUBLANES
    last = n // SUBLANES - 1
    return pl.pallas_call(
        functools.partial(_in_proj_body, segs),
        grid=(n // tm,),
        in_specs=[
            pl.BlockSpec((SUBLANES, d), lambda i: (jnp.maximum(i * hb - 1, 0), 0)),
            pl.BlockSpec((tm, d), lambda i: (i, 0)),
            pl.BlockSpec((SUBLANES, d), lambda i: (jnp.minimum((i + 1) * hb, last), 0)),
            pl.BlockSpec((1, d), lambda i: (0, 0)),
            pl.BlockSpec((d, HG_COLS), lambda i: (0, 0)),
            pl.BlockSpec((d, RW_COLS), lambda i: (0, 0)),
            pl.BlockSpec((1, RW_COLS), lambda i: (0, 0)),
        ],
        out_specs=[
            pl.BlockSpec((tm, HG_COLS), lambda i: (i, 0)),
            pl.BlockSpec((tm, RW_COLS), lambda i: (i, 0)),
        ],
        out_shape=[jax.ShapeDtypeStruct((n, HG_COLS), BF16), jax.ShapeDtypeStruct((n, RW_COLS), BF16)],
        compiler_params=_params(),
        name="in_proj",
    )(x, x, x, norm_g, w_hg, w_rw, mu)


def _tri(c, reverse):
    t = lax.broadcasted_iota(jnp.int32, (c, c), 0)
    s = lax.broadcasted_iota(jnp.int32, (c, c), 1)
    return (s >= t) if reverse else (s <= t)


def _hgrn2_body(segs, reverse, nblk, *refs):
    if reverse:
        q_ref, f_ref, i_ref, lb_ref, out_ref, st_ref = refs
    else:
        q_ref, f_ref, i_ref, g_ref, ob_ref, lb_ref, ng_ref, out_ref, st_ref = refs
    tb = q_ref.shape[0]
    c = HG_CHUNK
    step = pl.program_id(0)
    blk = (nblk - 1 - step) if reverse else step
    edge = (blk + 1) * tb if reverse else blk * tb

    @pl.when(_is_seq_boundary(segs, edge))
    def _():
        st_ref[...] = jnp.zeros_like(st_ref)

    lbp = lb_ref[...]
    e0 = jnp.exp(lbp[0:1] - jnp.maximum(lbp[0:1], lbp[1:2]))
    e1 = jnp.exp(lbp[1:2] - jnp.maximum(lbp[0:1], lbp[1:2]))
    lb = e0 / (e0 + e1)
    tri = _tri(c, reverse)
    tri_b = tri.astype(BF16)
    mid = c // 2 if reverse else c // 2 - 1
    last = 0 if reverse else c - 1
    nchunk = tb // c
    order = range(nchunk - 1, -1, -1) if reverse else range(nchunk)
    for ci in order:
        rs = slice(ci * c, (ci + 1) * c)
        q = q_ref[rs, :].astype(F32)
        f = lb + (1.0 - lb) * _sigmoid(f_ref[rs, :].astype(F32))
        kx = 1.0 - f
        cum = _dot_exact_lhs(tri_b, jnp.log(f))
        rho = cum[mid:mid + 1]
        tot = cum[last:last + 1]
        qs = q * _sigmoid(q)
        qh = (qs * jnp.exp(jnp.minimum(cum - rho, EXP_CLAMP))).astype(BF16)
        kh = (kx * jnp.exp(jnp.minimum(rho - cum, EXP_CLAMP))).astype(BF16)
        qt = (qs * jnp.exp(cum)).astype(BF16)
        ke = (kx * jnp.exp(tot - cum)).astype(BF16)
        vv = i_ref[rs, :]
        etot = jnp.exp(tot)
        outs = []
        for h in range(HG_HEADS):
            hs = slice(h * HG_HEAD_DIM, (h + 1) * HG_HEAD_DIM)
            st = st_ref[h]
            sc = jnp.where(tri, _dot_nt(qh[:, hs], kh[:, hs]), 0.0)
            o = _dot(sc.astype(BF16), vv[:, hs]) + _dot_nt(qt[:, hs], st.astype(BF16))
            st_ref[h] = st * etot[:, hs] + _dot_tn(vv[:, hs], ke[:, hs])
            outs.append(o)
        if reverse:
            for h in range(HG_HEADS):
                out_ref[rs, h * HG_HEAD_DIM:(h + 1) * HG_HEAD_DIM] = outs[h].astype(out_ref.dtype)
        else:
            ng = ng_ref[...]
            gg = g_ref[rs, :].astype(F32)
            gate = gg * _sigmoid(gg)
            for h in range(HG_HEADS):
                hs = slice(h * HG_HEAD_DIM, (h + 1) * HG_HEAD_DIM)
                o = outs[h] + ob_ref[rs, hs].astype(F32)
                o = o * lax.rsqrt(jnp.mean(o * o, axis=-1, keepdims=True) + NORM_EPS) * ng[:, hs]
                out_ref[rs, hs] = (o * gate[:, hs]).astype(out_ref.dtype)


def _hgrn2(hg, lower_bound, norm_g, segs):
    n = hg.shape[0]
    tb = HG_BLOCK
    nblk = n // tb
    w = HG_WIDTH
    scratch = [pltpu.VMEM((HG_HEADS, HG_HEAD_DIM, HG_HEAD_DIM), F32)]

    def col(j, rev):
        if rev:
            return pl.BlockSpec((tb, w), lambda i: (nblk - 1 - i, j))
        return pl.BlockSpec((tb, w), lambda i: (i, j))

    o_b = pl.pallas_call(
        functools.partial(_hgrn2_body, segs, True, nblk),
        grid=(nblk,),
        in_specs=[col(0, True), col(2, True), col(3, True), pl.BlockSpec((2, w), lambda i: (0, 0))],
        out_specs=col(0, True),
        out_shape=jax.ShapeDtypeStruct((n, w), BF16),
        scratch_shapes=scratch,
        compiler_params=_params(),
        name="hgrn2_bwd",
    )(hg, hg, hg, lower_bound)
    return pl.pallas_call(
        functools.partial(_hgrn2_body, segs, False, nblk),
        grid=(nblk,),
        in_specs=[col(0, False), col(1, False), col(3, False), col(4, False), col(0, False),
                  pl.BlockSpec((2, w), lambda i: (0, 0)), pl.BlockSpec((1, w), lambda i: (0, 0))],
        out_specs=col(0, False),
        out_shape=jax.ShapeDtypeStruct((n, w), BF16),
        scratch_shapes=scratch,
        compiler_params=_params(),
        name="hgrn2_fwd",
    )(hg, hg, hg, hg, o_b, lower_bound, norm_g)


def _block_diag_mask(n, blk):
    shift = blk.bit_length() - 1
    assert 1 << shift == blk
    r = lax.broadcasted_iota(jnp.int32, (n, n), 0) >> shift
    c = lax.broadcasted_iota(jnp.int32, (n, n), 1) >> shift
    return r == c


def _rwkv7_body(segs, reverse, nblk, *refs):
    if reverse:
        (r_ref, k_ref, v_ref, t_ref, w0_ref, w2h_ref, w2l_ref, a0_ref, a2_ref, kk_ref, ka_ref,
         out_ref, s_ref) = refs
    else:
        (r_ref, k_ref, v_ref, t_ref, ob_ref, w0_ref, w2h_ref, w2l_ref, a0_ref, a2_ref, a0b_ref, a2b_ref,
         g2_ref, kk_ref, ka_ref, rk_ref, lng_ref, lnb_ref, out_ref, s_ref) = refs
    tb = r_ref.shape[0]
    c = RW_CHUNK
    gw = RW_GROUP
    ngroup = RW_WIDTH // gw
    hpg = gw // RW_HEAD_DIM
    step = pl.program_id(0)
    blk = (nblk - 1 - step) if reverse else step
    edge = (blk + 1) * tb if reverse else blk * tb

    @pl.when(_is_seq_boundary(segs, edge))
    def _():
        s_ref[...] = jnp.zeros_like(s_ref)

    tri = _tri(c, reverse)
    tri_b = tri.astype(BF16)
    bd = _block_diag_mask(gw, RW_HEAD_DIM)
    bd_b = bd.astype(BF16)
    head_ones = _block_diag_mask(RW_WIDTH, RW_HEAD_DIM).astype(BF16)
    tt = lax.broadcasted_iota(jnp.int32, (c, gw), 0)
    ss = lax.broadcasted_iota(jnp.int32, (c, gw), 1) & (c - 1)
    m_strict = (ss > tt) if reverse else (ss < tt)
    m_incl = (ss >= tt) if reverse else (ss <= tt)
    eye = (ss == tt).astype(F32)
    last = 0 if reverse else c - 1

    def blockdiag(x):
        return jnp.concatenate([x.astype(BF16)] * hpg, axis=0) * bd_b

    def headsum(x):
        return _dot_exact_lhs_t(x, head_ones)

    w0 = w0_ref[...]
    a0 = a0_ref[...]
    k_k = kk_ref[...]
    k_a = ka_ref[...]
    nchunk = tb // c
    order = range(nchunk - 1, -1, -1) if reverse else range(nchunk)
    for ci in order:
        rs = slice(ci * c, (ci + 1) * c)
        r = r_ref[rs, :].astype(F32)
        k = k_ref[rs, :].astype(F32)
        vb = v_ref[rs, :]
        lat = t_ref[rs, 0:2 * RW_DECAY_RANK].astype(F32)
        wpre = w0 + _dot3(jnp.tanh(lat), w2h_ref[...], w2l_ref[...])
        sp = jnp.maximum(-wpre, 0.0) + jnp.log(1.0 + jnp.exp(-jnp.abs(wpre)))
        lw = -jnp.exp(-sp - 0.5)
        a_lr = _sigmoid(a0 + _dot(lat.astype(BF16), a2_ref[...]))
        kkr = k * k_k
        kkn = kkr / jnp.maximum(jnp.sqrt(headsum(kkr * kkr)), 1e-12)
        kd = k * (1.0 + (a_lr - 1.0) * k_a)
        cum = _dot_exact_lhs(tri_b, lw)
        ex = cum - lw
        tot = cum[last:last + 1]
        ecn = jnp.exp(-cum)
        rt = r * jnp.exp(cum)
        at = -kkn * jnp.exp(ex)
        bvec = kkn * a_lr
        bt = bvec * ecn
        kt = kd * ecn
        eend = jnp.exp(tot - cum)
        bh = bvec * eend
        kh = kd * eend
        etot = jnp.exp(tot)
        o_groups = []
        for g in range(ngroup):
            gs = slice(g * gw, (g + 1) * gw)
            ar = jnp.concatenate([at[:, gs], rt[:, gs]], axis=0).astype(BF16)
            g1 = _dot_nt(ar, blockdiag(bt[:, gs]))
            g2 = _dot_nt(ar, blockdiag(kt[:, gs]))
            n_ab = jnp.where(m_strict, g1[:c], 0.0)
            n_ak = jnp.where(m_strict, g2[:c], 0.0)
            m_rb = jnp.where(m_incl, g1[c:], 0.0)
            m_rk = jnp.where(m_incl, g2[c:], 0.0)
            tinv = eye + n_ab
            pw = n_ab
            nsq = c.bit_length() - 1
            for j in range(1, nsq):
                pw = _dot(pw.astype(BF16), blockdiag(pw))
                tinv = tinv + _dot(pw.astype(BF16), blockdiag(tinv))
            s_g = s_ref[g]
            ars = _dot_nt(ar, s_g.astype(BF16))
            vbd = blockdiag(vb[:, gs])
            rhs = ars[:c] + _dot(n_ak.astype(BF16), vbd)
            u = _dot(tinv.astype(BF16), blockdiag(rhs))
            o = ars[c:] + _dot(m_rb.astype(BF16), blockdiag(u)) + _dot(m_rk.astype(BF16), vbd)
            uv = jnp.concatenate([u.astype(BF16), vb[:, gs]], axis=0)
            bk = jnp.concatenate([bh[:, gs], kh[:, gs]], axis=0).astype(BF16)
            s_ref[g] = s_g * etot[:, gs] + jnp.where(bd, _dot_tn(uv, bk), 0.0)
            o_groups.append(o)
        if reverse:
            for g in range(ngroup):
                out_ref[rs, g * gw:(g + 1) * gw] = o_groups[g].astype(out_ref.dtype)
        else:
            o = jnp.concatenate(o_groups, axis=1) + ob_ref[rs, :].astype(F32)
            inv_n = 1.0 / RW_HEAD_DIM
            mean = headsum(o) * inv_n
            d = o - mean
            var = headsum(d * d) * inv_n
            o = d * lax.rsqrt(var + RW_LN_EPS) * lng_ref[...] + lnb_ref[...]
            a_lr_b = _sigmoid(a0b_ref[...] + _dot(lat.astype(BF16), a2b_ref[...]))
            kd_b = k * (1.0 + (a_lr_b - 1.0) * k_a)
            bonus = headsum(r * (kd + kd_b) * rk_ref[...]) * vb.astype(F32)
            gate = _dot(_sigmoid(t_ref[rs, 2 * RW_DECAY_RANK:].astype(F32)).astype(BF16), g2_ref[...])
            out_ref[rs, :] = ((o + bonus) * gate).astype(out_ref.dtype)


def _dot_exact_lhs_t(x, ones_bf16):
    hi, lo = _split(x)
    return _dot(hi, ones_bf16) + _dot(lo, ones_bf16)


def _rwkv7(rw, p, segs):
    n = rw.shape[0]
    tb = RW_BLOCK
    nblk = n // tb
    w = RW_WIDTH
    scratch = [pltpu.VMEM((w // RW_GROUP, RW_GROUP, RW_GROUP), F32)]

    def rows(j, width, rev):
        if rev:
            return pl.BlockSpec((tb, width), lambda i: (nblk - 1 - i, j))
        return pl.BlockSpec((tb, width), lambda i: (i, j))

    def full(a):
        return pl.BlockSpec(a.shape, lambda i: (0,) * a.ndim)

    tail_blk = (3 * w) // RW_TAIL
    bwd_params = [p["w0_b"], p["w2h_b"], p["w2l_b"], p["a0_b"], p["a2_b"], p["k_k"], p["k_a"]]
    o_b = pl.pallas_call(
        functools.partial(_rwkv7_body, segs, True, nblk),
        grid=(nblk,),
        in_specs=[rows(0, w, True), rows(1, w, True), rows(2, w, True), rows(tail_blk, RW_TAIL, True)]
        + [full(a) for a in bwd_params],
        out_specs=rows(0, w, True),
        out_shape=jax.ShapeDtypeStruct((n, w), BF16),
        scratch_shapes=scratch,
        compiler_params=_params(),
        name="rwkv7_bwd",
    )(rw, rw, rw, rw, *bwd_params)
    fwd_params = [p["w0_f"], p["w2h_f"], p["w2l_f"], p["a0_f"], p["a2_f"], p["a0_b"], p["a2_b"], p["g2"],
                  p["k_k"], p["k_a"], p["r_k"], p["ln_g"], p["ln_b"]]
    return pl.pallas_call(
        functools.partial(_rwkv7_body, segs, False, nblk),
        grid=(nblk,),
        in_specs=[rows(0, w, False), rows(1, w, False), rows(2, w, False), rows(tail_blk, RW_TAIL, False),
                  rows(0, w, False)] + [full(a) for a in fwd_params],
        out_specs=rows(0, w, False),
        out_shape=jax.ShapeDtypeStruct((n, w), BF16),
        scratch_shapes=scratch,
        compiler_params=_params(),
        name="rwkv7_fwd",
    )(rw, rw, rw, rw, o_b, *fwd_params)


def _store_row_tiles(ref, val):
    rows = val.shape[0]
    for j in range(ROW_TILE):
        ref[pl.ds(j, rows, stride=ROW_TILE), :] = val[:, j * LANES:(j + 1) * LANES]


def _load_row_tiles(ref, rows):
    return jnp.concatenate([ref[pl.ds(j, rows, stride=ROW_TILE), :] for j in range(ROW_TILE)], axis=1)


def _out_router_body(x_ref, yh_ref, yr_ref, woa_ref, wob_ref, g_ref, rwh_ref, rwl_ref, rb_ref,
                     x2_ref, h2_ref, idx_ref, gate_ref, rank_ref, cnt_ref, carry_ref):
    tm = x_ref.shape[0]

    @pl.when(pl.program_id(0) == 0)
    def _():
        carry_ref[...] = jnp.zeros_like(carry_ref)

    x2 = x_ref[...] + _dot(yh_ref[...], woa_ref[...]) + _dot(yr_ref[...], wob_ref[...])
    x2_ref[...] = x2
    h2 = x2 * lax.rsqrt(jnp.mean(x2 * x2, axis=-1, keepdims=True) + NORM_EPS) * g_ref[...]
    _store_row_tiles(h2_ref, h2)
    hi, lo = _split(h2)
    logits = (_dot_nt(rwh_ref[...], hi) + _dot_nt(rwh_ref[...], lo) + _dot_nt(rwl_ref[...], hi)) + rb_ref[...]
    ids = lax.broadcasted_iota(jnp.int32, logits.shape, 0)
    vals = logits
    tops, picks = [], []
    for _ in range(TOP_K):
        m = jnp.max(vals, axis=0, keepdims=True)
        pick = jnp.min(jnp.where(vals == m, ids, N_EXPERTS), axis=0, keepdims=True)
        vals = jnp.where(ids == pick, -jnp.inf, vals)
        tops.append(m)
        picks.append(pick)
    es = [jnp.exp(t - tops[0]) for t in tops]
    den = es[0] + es[1] + es[2] + es[3]
    idx_ref[...] = jnp.concatenate(picks, axis=0)
    gate_ref[...] = jnp.concatenate([e / den for e in es], axis=0)
    onehots = [(ids == pick).astype(F32) for pick in picks]
    member = (onehots[0] + onehots[1]) + (onehots[2] + onehots[3])
    t0 = lax.broadcasted_iota(jnp.int32, (tm, tm), 0)
    t1 = lax.broadcasted_iota(jnp.int32, (tm, tm), 1)
    earlier = _dot(member.astype(BF16), (t0 < t1).astype(BF16))
    base = carry_ref[:, 0:1] + earlier
    rank_ref[...] = jnp.concatenate([jnp.sum(oh * base, axis=0, keepdims=True) for oh in onehots],
                                    axis=0).astype(jnp.int32)
    carry = carry_ref[...] + jnp.sum(member, axis=1, keepdims=True)
    carry_ref[...] = carry
    cnt_ref[...] = carry


def _out_router(x, y_hg, y_rw, wo_a, wo_b, norm_g, rw_hi, rw_lo, rb):
    n, d = x.shape
    tm = PROJ_TILE

    def full(a):
        return pl.BlockSpec(a.shape, lambda i: (0,) * a.ndim)

    return pl.pallas_call(
        _out_router_body,
        grid=(n // tm,),
        in_specs=[pl.BlockSpec((tm, d), lambda i: (i, 0)),
                  pl.BlockSpec((tm, HG_WIDTH), lambda i: (i, 0)),
                  pl.BlockSpec((tm, RW_WIDTH), lambda i: (i, 0)),
                  full(wo_a), full(wo_b), full(norm_g), full(rw_hi), full(rw_lo), full(rb)],
        out_specs=[pl.BlockSpec((tm, d), lambda i: (i, 0)),
                   pl.BlockSpec((tm * ROW_TILE, LANES), lambda i: (i, 0)),
                   pl.BlockSpec((TOP_K, tm), lambda i: (0, i)),
                   pl.BlockSpec((TOP_K, tm), lambda i: (0, i)),
                   pl.BlockSpec((TOP_K, tm), lambda i: (0, i)),
                   pl.BlockSpec((N_EXPERTS, LANES), lambda i: (0, 0))],
        out_shape=[jax.ShapeDtypeStruct((n, d), F32), jax.ShapeDtypeStruct((n * ROW_TILE, LANES), F32),
                   jax.ShapeDtypeStruct((TOP_K, n), jnp.int32), jax.ShapeDtypeStruct((TOP_K, n), F32),
                   jax.ShapeDtypeStruct((TOP_K, n), jnp.int32),
                   jax.ShapeDtypeStruct((N_EXPERTS, LANES), F32)],
        scratch_shapes=[pltpu.VMEM((N_EXPERTS, LANES), F32)],
        compiler_params=_params(),
        name="out_router",
    )(x, y_hg, y_rw, wo_a, wo_b, norm_g, rw_hi, rw_lo, rb)


def _tile_copy(src_hbm, row, dst_vmem, slot, sem):
    return pltpu.make_async_copy(src_hbm.at[pl.ds(pl.multiple_of(row * ROW_TILE, ROW_TILE), ROW_TILE)],
                                 dst_vmem.at[pl.ds(slot * ROW_TILE, ROW_TILE)], sem)


def _moe_body(be_ref, nu_ref, tok_ref, tokn_ref, h_hbm, wgu_ref, bgu_ref, wdn_ref, bdn_ref, out_ref,
              xbuf, sem):
    b = pl.program_id(0)
    nu = nu_ref[0]
    blk = MOE_BLOCK
    cur = lax.rem(b, 2)

    def gather(toks, s):
        for r in range(blk):
            _tile_copy(h_hbm, toks[0, 0, r], xbuf.at[s], r, sem.at[s]).start()

    def drain(s):
        for r in range(blk):
            _tile_copy(h_hbm, 0, xbuf.at[s], r, sem.at[s]).wait()

    @pl.when(jnp.logical_and(b == 0, nu > 0))
    def _():
        gather(tok_ref, 0)

    @pl.when(b < nu)
    def _():
        drain(cur)
        gather(tokn_ref, 1 - cur)
        x = _load_row_tiles(xbuf.at[cur], blk).astype(BF16)
        gu = _dot(x, wgu_ref[0]) + bgu_ref[0]
        glu = jnp.minimum(gu[:, :D_FF], SWIGLU_LIMIT)
        lin = jnp.clip(gu[:, D_FF:], -SWIGLU_LIMIT, SWIGLU_LIMIT)
        act = glu * _sigmoid(SWIGLU_ALPHA * glu) * (lin + 1.0)
        _store_row_tiles(out_ref, _dot(act.astype(BF16), wdn_ref[0]) + bdn_ref[0])

        @pl.when(b == nu - 1)
        def _():
            drain(1 - cur)

    @pl.when(b >= nu)
    def _():
        out_ref[...] = jnp.zeros_like(out_ref)


def _moe(h2_tiles, blk_expert, n_used, row_tok, w_gu, b_gu, w_dn, b_dn):
    d = D_MODEL
    blk = MOE_BLOCK
    nblk = row_tok.shape[0]
    grid_spec = pltpu.PrefetchScalarGridSpec(
        num_scalar_prefetch=2,
        grid=(nblk,),
        in_specs=[
            pl.BlockSpec((1, 1, blk), lambda b, be, nu: (b, 0, 0), memory_space=pltpu.SMEM),
            pl.BlockSpec((1, 1, blk), lambda b, be, nu: (jnp.minimum(b + 1, nblk - 1), 0, 0),
                         memory_space=pltpu.SMEM),
            pl.BlockSpec(memory_space=pl.ANY),
            pl.BlockSpec((1, d, 2 * D_FF), lambda b, be, nu: (be[b], 0, 0)),
            pl.BlockSpec((1, 1, 2 * D_FF), lambda b, be, nu: (be[b], 0, 0)),
            pl.BlockSpec((1, D_FF, d), lambda b, be, nu: (be[b], 0, 0)),
            pl.BlockSpec((1, 1, d), lambda b, be, nu: (be[b], 0, 0)),
        ],
        out_specs=pl.BlockSpec((blk * ROW_TILE, LANES), lambda b, be, nu: (b, 0)),
        scratch_shapes=[pltpu.VMEM((2, blk * ROW_TILE, LANES), F32), pltpu.SemaphoreType.DMA((2,))],
    )
    return pl.pallas_call(
        _moe_body,
        grid_spec=grid_spec,
        out_shape=jax.ShapeDtypeStruct((nblk * blk * ROW_TILE, LANES), F32),
        compiler_params=_params(),
        name="moe_experts",
    )(blk_expert, n_used, row_tok, row_tok, h2_tiles, w_gu, b_gu, w_dn, b_dn)


def _combine_body(pos_ref, posn_ref, x2_ref, gate_ref, g_ref, yb_hbm, out_ref, ybuf, sem):
    tm = x2_ref.shape[0]
    i = pl.program_id(0)
    cur = lax.rem(i, 2)

    def gather(pos, s):
        for t in range(tm):
            for k in range(TOP_K):
                _tile_copy(yb_hbm, pos[k, t], ybuf.at[s, k], t, sem.at[s]).start()

    def drain(s):
        for t in range(tm):
            for k in range(TOP_K):
                _tile_copy(yb_hbm, 0, ybuf.at[s, k], t, sem.at[s]).wait()

    @pl.when(i == 0)
    def _():
        gather(pos_ref, 0)

    @pl.when(i + 1 < pl.num_programs(0))
    def _():
        gather(posn_ref, 1 - cur)

    drain(cur)
    rr = lax.broadcasted_iota(jnp.int32, (tm, tm), 0)
    cc = lax.broadcasted_iota(jnp.int32, (tm, tm), 1)
    x = x2_ref[...]
    for k in range(TOP_K):
        gcol = jnp.sum(jnp.where(rr == cc, gate_ref[k:k + 1, :], 0.0), axis=1, keepdims=True)
        x = x + gcol * _load_row_tiles(ybuf.at[cur, k], tm)
    out_ref[...] = x * lax.rsqrt(jnp.mean(x * x, axis=-1, keepdims=True) + NORM_EPS) * g_ref[...]


def _combine(pos_t, gate_t, x2, final_g, yb_tiles):
    n, d = x2.shape
    tm = COMBINE_TILE
    last = n // tm - 1
    return pl.pallas_call(
        _combine_body,
        grid=(n // tm,),
        in_specs=[pl.BlockSpec((TOP_K, tm), lambda i: (0, i), memory_space=pltpu.SMEM),
                  pl.BlockSpec((TOP_K, tm), lambda i: (0, jnp.minimum(i + 1, last)), memory_space=pltpu.SMEM),
                  pl.BlockSpec((tm, d), lambda i: (i, 0)),
                  pl.BlockSpec((TOP_K, tm), lambda i: (0, i)),
                  pl.BlockSpec((1, d), lambda i: (0, 0)),
                  pl.BlockSpec(memory_space=pl.ANY)],
        out_specs=pl.BlockSpec((tm, d), lambda i: (i, 0)),
        out_shape=jax.ShapeDtypeStruct((n, d), F32),
        scratch_shapes=[pltpu.VMEM((2, TOP_K, tm * ROW_TILE, LANES), F32), pltpu.SemaphoreType.DMA((2,))],
        compiler_params=_params(),
        name="moe_combine",
    )(pos_t, pos_t, x2, gate_t, final_g, yb_tiles)


def _routing_tables(idx_t, rank_t, counts):
    n = idx_t.shape[1]
    n_assign = n * TOP_K
    blk = MOE_BLOCK
    counts = counts.astype(jnp.int32)
    padded = (counts + blk - 1) // blk * blk
    grp_start = jnp.cumsum(counts) - counts
    pad_end = jnp.cumsum(padded)
    pad_start = pad_end - padded
    onehot = idx_t[..., None] == jnp.arange(N_EXPERTS, dtype=jnp.int32)
    pos_t = rank_t + jnp.sum(jnp.where(onehot, pad_start, 0), axis=-1)
    keys = jnp.sort((idx_t * n + jnp.arange(n, dtype=jnp.int32)[None]).reshape(-1))
    nblk = (n_assign + N_EXPERTS * (blk - 1)) // blk + 1
    blk_expert = jnp.minimum(
        jnp.searchsorted(pad_end, jnp.arange(nblk, dtype=jnp.int32) * blk, side="right"),
        N_EXPERTS - 1).astype(jnp.int32)
    within = (jnp.arange(nblk, dtype=jnp.int32) * blk - pad_start[blk_expert])[:, None] \
        + jnp.arange(blk, dtype=jnp.int32)[None]
    valid = within < counts[blk_expert][:, None]
    src = jnp.clip(grp_start[blk_expert][:, None] + within, 0, n_assign - 1)
    row_tok = jnp.where(valid, keys[src] - blk_expert[:, None] * n, 0)
    n_used = (pad_end[-1] // blk).astype(jnp.int32).reshape(1)
    return blk_expert, n_used, row_tok.reshape(nblk, 1, blk), pos_t


def _pad_rows(w, top, total):
    return jnp.zeros((total, w.shape[1]), w.dtype).at[top:top + w.shape[0]].set(w)


def _encoder(x, segs, norm_mix_g, w_in, hg_lower_bound, hg_norm_g, rw_mu, rw_w0, rw_w2, rw_a0, rw_a2, rw_g2,
             rw_k_k, rw_k_a, rw_r_k, rw_ln_g, rw_ln_b, w_out, norm_ffn_g, router_w, router_b, exp_w_gate_up,
             exp_b_gate_up, exp_w_down, exp_b_down, final_norm_g):
    w_in_b = w_in[0].astype(BF16)
    hg, rw = _in_proj(x, norm_mix_g, w_in_b[:, :HG_COLS], w_in_b[:, HG_COLS:], rw_mu, segs)
    y_hg = _hgrn2(hg, hg_lower_bound, hg_norm_g, segs)
    lat = 2 * RW_DECAY_RANK
    p = {"k_k": rw_k_k, "k_a": rw_k_a, "r_k": rw_r_k, "ln_g": rw_ln_g, "ln_b": rw_ln_b,
         "g2": rw_g2[0].astype(BF16)}
    for d, tag in ((0, "f"), (1, "b")):
        w2 = _pad_rows(rw_w2[0, d], 0, lat)
        p["w2h_" + tag], p["w2l_" + tag] = _split(w2)
        p["a2_" + tag] = _pad_rows(rw_a2[0, d], RW_DECAY_RANK, lat).astype(BF16)
        p["w0_" + tag] = rw_w0[0, d][None]
        p["a0_" + tag] = rw_a0[0, d][None]
    y_rw = _rwkv7(rw, p, segs)
    wo = w_out[0].astype(BF16)
    rw_hi, rw_lo = _split(router_w[0].T)
    x2, h2, idx_t, gate_t, rank_t, counts = _out_router(x, y_hg, y_rw, wo[:HG_WIDTH], wo[HG_WIDTH:], norm_ffn_g,
                                                        rw_hi, rw_lo, router_b[0][:, None])
    blk_expert, n_used, row_tok, pos_t = _routing_tables(idx_t, rank_t, counts[:, 0])
    yb = _moe(h2, blk_expert, n_used, row_tok, exp_w_gate_up[0].astype(BF16),
              exp_b_gate_up[0][:, None, :], exp_w_down[0].astype(BF16), exp_b_down[0][:, None, :])
    return _combine(pos_t, gate_t, x2, final_norm_g[None], yb)


def kernel(x_prompt, x_sample, norm_mix_g, w_in, hg_lower_bound, hg_norm_g, rw_mu, rw_w0, rw_w2, rw_a0, rw_a2,
           rw_g2, rw_k_k, rw_k_a, rw_r_k, rw_ln_g, rw_ln_b, w_out, norm_ffn_g, router_w, router_b, exp_w_gate_up,
           exp_b_gate_up, exp_w_down, exp_b_down, final_norm_g):
    d = x_prompt.shape[-1]
    segs = (x_prompt.shape[:2], x_sample.shape[:2])
    n_p = x_prompt.shape[0] * x_prompt.shape[1]
    x = jnp.concatenate([x_prompt.reshape(-1, d), x_sample.reshape(-1, d)], axis=0)
    y = _encoder(x, segs, norm_mix_g, w_in, hg_lower_bound, hg_norm_g, rw_mu, rw_w0, rw_w2, rw_a0, rw_a2,
                 rw_g2, rw_k_k, rw_k_a, rw_r_k, rw_ln_g, rw_ln_b, w_out, norm_ffn_g, router_w, router_b,
                 exp_w_gate_up, exp_b_gate_up, exp_w_down, exp_b_down, final_norm_g)
    return y[:n_p].reshape(x_prompt.shape), y[n_p:].reshape(x_sample.shape)
```

```python
import functools

import jax
import jax.numpy as jnp
from jax import lax
from jax.experimental import pallas as pl
from jax.experimental.pallas import tpu as pltpu

F32 = jnp.float32
BF16 = jnp.bfloat16

D_MODEL = 1024
HG_WIDTH = 512
HG_HEAD_DIM = 128
HG_HEADS = HG_WIDTH // HG_HEAD_DIM
RW_WIDTH = 512
RW_HEAD_DIM = 64
RW_DECAY_RANK = 64
RW_AAA_RANK = 64
RW_GATE_RANK = 128
HG_COLS = 5 * HG_WIDTH
RW_COLS = 3 * RW_WIDTH + RW_DECAY_RANK + RW_AAA_RANK + RW_GATE_RANK
RW_TAIL = RW_COLS - 3 * RW_WIDTH
N_EXPERTS = 32
TOP_K = 4
D_FF = D_MODEL
SWIGLU_LIMIT = 7.0
SWIGLU_ALPHA = 1.702
NORM_EPS = 1e-5
RW_LN_EPS = 64e-5

V7X_VMEM_LIMIT_BYTES = 56 * 1024 * 1024
SUBLANES = 8
LANES = 128
ROW_TILE = D_MODEL // LANES
assert ROW_TILE == SUBLANES

PROJ_TILE = 256
HG_CHUNK = 64
HG_BLOCK = 256
RW_CHUNK = 64
RW_GROUP = 256
RW_BLOCK = 512
RW_LANES = 6
MOE_BLOCK = 256
COMBINE_TILE = 128
EXP_CLAMP = 80.0


def _dot(a, b):
    return jnp.dot(a, b, preferred_element_type=F32)


def _dot_nt(a, b):
    return lax.dot_general(a, b, (((1,), (1,)), ((), ())), preferred_element_type=F32)


def _dot_tn(a, b):
    return lax.dot_general(a, b, (((0,), (0,)), ((), ())), preferred_element_type=F32)


def _split(x):
    hi = x.astype(BF16)
    lo = (x - hi.astype(F32)).astype(BF16)
    return hi, lo


def _dot_exact_lhs(a_bf16, x):
    hi, lo = _split(x)
    return _dot(a_bf16, hi) + _dot(a_bf16, lo)


def _dot3(x, w_hi, w_lo):
    hi, lo = _split(x)
    return _dot(hi, w_hi) + _dot(lo, w_hi) + _dot(hi, w_lo)


def _sigmoid(x):
    return 1.0 / (1.0 + jnp.exp(-x))


def _is_seq_boundary(segs, row):
    base = 0
    hit = row == sum(n * t for n, t in segs)
    for n, t in segs:
        inside = jnp.logical_and(row >= base, row < base + n * t)
        hit = jnp.logical_or(hit, jnp.logical_and(inside, lax.rem(row - base, t) == 0))
        base += n * t
    return hit


def _params(vmem=V7X_VMEM_LIMIT_BYTES):
    return pltpu.CompilerParams(dimension_semantics=("arbitrary",), vmem_limit_bytes=vmem)


def _two_array_specs(block, n_a, n_b, shift=0, unit=None):
    rows = block[0]
    per = (unit or rows) // rows
    nba, nbb = n_a // rows, n_b // rows

    def idx_a(i):
        return (jnp.clip(i * per + shift, 0, nba - 1), 0)

    def idx_b(i):
        return (jnp.clip(i * per + shift - nba, 0, nbb - 1), 0)

    return pl.BlockSpec(block, idx_a), pl.BlockSpec(block, idx_b)


def _pick(in_a, ref_a, ref_b):
    return jnp.where(in_a, ref_a[...], ref_b[...])


def _in_proj_body(segs, xpa_ref, xpb_ref, xa_ref, xb_ref, xna_ref, xnb_ref, g_ref, whg_ref, wrw_ref, mu_ref,
                  hg_ref, rw_ref):
    tm = xa_ref.shape[0]
    n_a = segs[0][0] * segs[0][1]
    row0 = pl.program_id(0) * tm
    at_start = _is_seq_boundary(segs, row0)
    at_end = _is_seq_boundary(segs, row0 + tm)
    g = g_ref[...]

    def nrm(x):
        return x * lax.rsqrt(jnp.mean(x * x, axis=-1, keepdims=True) + NORM_EPS) * g

    h = nrm(_pick(row0 < n_a, xa_ref, xb_ref))
    hp = jnp.where(at_start, 0.0, nrm(_pick(row0 - 1 < n_a, xpa_ref, xpb_ref)))
    hn = jnp.where(at_end, 0.0, nrm(_pick(row0 + tm < n_a, xna_ref, xnb_ref)))
    hb = h.astype(BF16)
    for c in range(HG_COLS // HG_WIDTH):
        cs = slice(c * HG_WIDTH, (c + 1) * HG_WIDTH)
        hg_ref[:, cs] = _dot(hb, whg_ref[:, cs]).astype(BF16)
    hs = jnp.concatenate([hp.astype(BF16), hb, hn.astype(BF16)], axis=0)
    p = _dot(hs, wrw_ref[...])
    rows = tm + 2 * SUBLANES
    cur = p[SUBLANES:SUBLANES + tm]
    prev = pltpu.roll(p, 1, 0)[SUBLANES:SUBLANES + tm]
    nxt = pltpu.roll(p, rows - 1, 0)[SUBLANES:SUBLANES + tm]
    mu = mu_ref[...]
    rw_ref[...] = (cur + mu * (0.5 * (prev + nxt) - cur)).astype(BF16)


def _in_proj(x_a, x_b, norm_g, w_hg, w_rw, mu, segs):
    d = x_a.shape[1]
    n_a, n_b = x_a.shape[0], x_b.shape[0]
    n = n_a + n_b
    tm = PROJ_TILE
    return pl.pallas_call(
        functools.partial(_in_proj_body, segs),
        grid=(n // tm,),
        in_specs=[
            *_two_array_specs((SUBLANES, d), n_a, n_b, shift=-1, unit=tm),
            *_two_array_specs((tm, d), n_a, n_b),
            *_two_array_specs((SUBLANES, d), n_a, n_b, shift=tm // SUBLANES, unit=tm),
            pl.BlockSpec((1, d), lambda i: (0, 0)),
            pl.BlockSpec((d, HG_COLS), lambda i: (0, 0)),
            pl.BlockSpec((d, RW_COLS), lambda i: (0, 0)),
            pl.BlockSpec((1, RW_COLS), lambda i: (0, 0)),
        ],
        out_specs=[
            pl.BlockSpec((tm, HG_COLS), lambda i: (i, 0)),
            pl.BlockSpec((tm, RW_COLS), lambda i: (i, 0)),
        ],
        out_shape=[jax.ShapeDtypeStruct((n, HG_COLS), BF16), jax.ShapeDtypeStruct((n, RW_COLS), BF16)],
        compiler_params=_params(),
        name="in_proj",
    )(x_a, x_b, x_a, x_b, x_a, x_b, norm_g, w_hg, w_rw, mu)


def _tri(c, reverse):
    t = lax.broadcasted_iota(jnp.int32, (c, c), 0)
    s = lax.broadcasted_iota(jnp.int32, (c, c), 1)
    return (s >= t) if reverse else (s <= t)


def _hgrn2_body(segs, reverse, nblk, *refs):
    if reverse:
        q_ref, f_ref, i_ref, lb_ref, out_ref, st_ref = refs
    else:
        q_ref, f_ref, i_ref, g_ref, ob_ref, lb_ref, ng_ref, out_ref, st_ref = refs
    tb = q_ref.shape[0]
    c = HG_CHUNK
    step = pl.program_id(0)
    blk = (nblk - 1 - step) if reverse else step
    edge = (blk + 1) * tb if reverse else blk * tb

    @pl.when(_is_seq_boundary(segs, edge))
    def _():
        st_ref[...] = jnp.zeros_like(st_ref)

    lbp = lb_ref[...]
    e0 = jnp.exp(lbp[0:1] - jnp.maximum(lbp[0:1], lbp[1:2]))
    e1 = jnp.exp(lbp[1:2] - jnp.maximum(lbp[0:1], lbp[1:2]))
    lb = e0 / (e0 + e1)
    tri = _tri(c, reverse)
    tri_b = tri.astype(BF16)
    mid = c // 2 if reverse else c // 2 - 1
    last = 0 if reverse else c - 1
    nchunk = tb // c
    order = list(range(nchunk - 1, -1, -1) if reverse else range(nchunk))
    heads = [slice(h * HG_HEAD_DIM, (h + 1) * HG_HEAD_DIM) for h in range(HG_HEADS)]
    gates = {ci: lb + (1.0 - lb) * _sigmoid(f_ref[ci * c:(ci + 1) * c, :].astype(F32)) for ci in order}
    cums = {ci: _dot_exact_lhs(tri_b, jnp.log(gates[ci])) for ci in order}
    cv = {}
    for ci in order:
        rs = slice(ci * c, (ci + 1) * c)
        q = q_ref[rs, :].astype(F32)
        kx = 1.0 - gates[ci]
        cum = cums[ci]
        rho = cum[mid:mid + 1]
        tot = cum[last:last + 1]
        qs = q * _sigmoid(q)
        qh = (qs * jnp.exp(jnp.minimum(cum - rho, EXP_CLAMP))).astype(BF16)
        kh = (kx * jnp.exp(jnp.minimum(rho - cum, EXP_CLAMP))).astype(BF16)
        ke = (kx * jnp.exp(tot - cum)).astype(BF16)
        vv = i_ref[rs, :]
        cv[ci] = dict(qt=(qs * jnp.exp(cum)).astype(BF16), etot=jnp.exp(tot), vv=vv,
                      sc=[jnp.where(tri, _dot_nt(qh[:, hs], kh[:, hs]), 0.0).astype(BF16) for hs in heads],
                      kv=[_dot_tn(vv[:, hs], ke[:, hs]) for hs in heads])
    states = [st_ref[h] for h in range(HG_HEADS)]
    for ci in order:
        rs = slice(ci * c, (ci + 1) * c)
        q = cv[ci]
        outs = []
        for h, hs in enumerate(heads):
            outs.append(_dot(q["sc"][h], q["vv"][:, hs]) + _dot_nt(q["qt"][:, hs], states[h].astype(BF16)))
            states[h] = states[h] * q["etot"][:, hs] + q["kv"][h]
        if ci == order[-1]:
            for h in range(HG_HEADS):
                st_ref[h] = states[h]
        if reverse:
            for h in range(HG_HEADS):
                out_ref[rs, h * HG_HEAD_DIM:(h + 1) * HG_HEAD_DIM] = outs[h].astype(out_ref.dtype)
        else:
            ng = ng_ref[...]
            gg = g_ref[rs, :].astype(F32)
            gate = gg * _sigmoid(gg)
            for h in range(HG_HEADS):
                hs = slice(h * HG_HEAD_DIM, (h + 1) * HG_HEAD_DIM)
                o = outs[h] + ob_ref[rs, hs].astype(F32)
                o = o * lax.rsqrt(jnp.mean(o * o, axis=-1, keepdims=True) + NORM_EPS) * ng[:, hs]
                out_ref[rs, hs] = (o * gate[:, hs]).astype(out_ref.dtype)


def _hgrn2(hg, lower_bound, norm_g, segs):
    n = hg.shape[0]
    tb = HG_BLOCK
    nblk = n // tb
    w = HG_WIDTH
    scratch = [pltpu.VMEM((HG_HEADS, HG_HEAD_DIM, HG_HEAD_DIM), F32)]

    def col(j, rev):
        if rev:
            return pl.BlockSpec((tb, w), lambda i: (nblk - 1 - i, j))
        return pl.BlockSpec((tb, w), lambda i: (i, j))

    o_b = pl.pallas_call(
        functools.partial(_hgrn2_body, segs, True, nblk),
        grid=(nblk,),
        in_specs=[col(0, True), col(2, True), col(3, True), pl.BlockSpec((2, w), lambda i: (0, 0))],
        out_specs=col(0, True),
        out_shape=jax.ShapeDtypeStruct((n, w), BF16),
        scratch_shapes=scratch,
        compiler_params=_params(),
        name="hgrn2_bwd",
    )(hg, hg, hg, lower_bound)
    return pl.pallas_call(
        functools.partial(_hgrn2_body, segs, False, nblk),
        grid=(nblk,),
        in_specs=[col(0, False), col(1, False), col(3, False), col(4, False), col(0, False),
                  pl.BlockSpec((2, w), lambda i: (0, 0)), pl.BlockSpec((1, w), lambda i: (0, 0))],
        out_specs=col(0, False),
        out_shape=jax.ShapeDtypeStruct((n, w), BF16),
        scratch_shapes=scratch,
        compiler_params=_params(),
        name="hgrn2_fwd",
    )(hg, hg, hg, hg, o_b, lower_bound, norm_g)


def _block_diag_mask(n, blk):
    shift = blk.bit_length() - 1
    assert 1 << shift == blk
    r = lax.broadcasted_iota(jnp.int32, (n, n), 0) >> shift
    c = lax.broadcasted_iota(jnp.int32, (n, n), 1) >> shift
    return r == c


def _rwkv7_body(segs, reverse, nblk, *refs):
    if reverse:
        (r_ref, k_ref, v_ref, t_ref, w0_ref, w2h_ref, w2l_ref, a0_ref, a2_ref, kk_ref, ka_ref,
         out_ref, s_ref) = refs
    else:
        (r_ref, k_ref, v_ref, t_ref, ob_ref, w0_ref, w2h_ref, w2l_ref, a0_ref, a2_ref, a0b_ref, a2b_ref,
         g2_ref, kk_ref, ka_ref, rk_ref, lng_ref, lnb_ref, out_ref, s_ref) = refs
    tb = r_ref.shape[0]
    c = RW_CHUNK
    gw = RW_GROUP
    ngroup = RW_WIDTH // gw
    hpg = gw // RW_HEAD_DIM
    step = pl.program_id(0)
    blk = (nblk - 1 - step) if reverse else step
    edge = (blk + 1) * tb if reverse else blk * tb

    @pl.when(_is_seq_boundary(segs, edge))
    def _():
        s_ref[...] = jnp.zeros_like(s_ref)

    tri = _tri(c, reverse)
    tri_b = tri.astype(BF16)
    bd = _block_diag_mask(gw, RW_HEAD_DIM)
    bd_b = bd.astype(BF16)
    tt = lax.broadcasted_iota(jnp.int32, (c, gw), 0)
    ss = lax.broadcasted_iota(jnp.int32, (c, gw), 1) & (c - 1)
    m_strict = (ss > tt) if reverse else (ss < tt)
    m_incl = (ss >= tt) if reverse else (ss <= tt)
    eye = (ss == tt).astype(F32)
    last = 0 if reverse else c - 1

    def blockdiag(x):
        return jnp.concatenate([x.astype(BF16)] * hpg, axis=0) * bd_b

    def headsum(x):
        rows = x.shape[0]
        hi, lo = _split(jnp.concatenate([x[:, g * gw:(g + 1) * gw] for g in range(ngroup)], axis=0))
        s = _dot(jnp.concatenate([hi, lo], axis=0), bd_b)
        s = s[:ngroup * rows] + s[ngroup * rows:]
        return jnp.concatenate([s[g * rows:(g + 1) * rows] for g in range(ngroup)], axis=1)

    k_a = ka_ref[...]
    r = r_ref[...].astype(F32)
    k = k_ref[...].astype(F32)
    vb = v_ref[...]
    lat = t_ref[:, 0:2 * RW_DECAY_RANK].astype(F32)
    latb = lat.astype(BF16)
    wpre = w0_ref[...] + _dot3(jnp.tanh(lat), w2h_ref[...], w2l_ref[...])
    sp = jnp.maximum(-wpre, 0.0) + jnp.log(1.0 + jnp.exp(-jnp.abs(wpre)))
    lw = -jnp.exp(-sp - 0.5)
    a_lr = _sigmoid(a0_ref[...] + _dot(latb, a2_ref[...]))
    kkr = k * kk_ref[...]
    kkn = kkr / jnp.maximum(jnp.sqrt(headsum(kkr * kkr)), 1e-12)
    kd = k * (1.0 + (a_lr - 1.0) * k_a)
    bvec = kkn * a_lr
    nchunk = tb // c
    nsq = c.bit_length() - 1

    order = list(range(nchunk - 1, -1, -1) if reverse else range(nchunk))
    chunk_vals = {}
    pre = {}
    o_parts = {}

    def chunk_prep(ci):
        rs = slice(ci * c, (ci + 1) * c)
        cum = _dot_exact_lhs(tri_b, lw[rs])
        tot = cum[last:last + 1]
        ecn = jnp.exp(-cum)
        eend = jnp.exp(tot - cum)
        chunk_vals[ci] = dict(rt=r[rs] * jnp.exp(cum), at=-kkn[rs] * jnp.exp(cum - lw[rs]),
                              bt=bvec[rs] * ecn, kt=kd[rs] * ecn, bh=bvec[rs] * eend, kh=kd[rs] * eend,
                              etot=jnp.exp(tot))

    def independent_task(ci, g):
        if ci not in chunk_vals:
            chunk_prep(ci)
            yield
        cv = chunk_vals[ci]
        rs = slice(ci * c, (ci + 1) * c)
        gs = slice(g * gw, (g + 1) * gw)
        ar = jnp.concatenate([cv["at"][:, gs], cv["rt"][:, gs]], axis=0).astype(BF16)
        g1 = _dot_nt(ar, blockdiag(cv["bt"][:, gs]))
        yield
        g2 = _dot_nt(ar, blockdiag(cv["kt"][:, gs]))
        yield
        n_ab = jnp.where(m_strict, g1[:c], 0.0)
        n_ak = jnp.where(m_strict, g2[:c], 0.0)
        m_rb = jnp.where(m_incl, g1[c:], 0.0)
        m_rk = jnp.where(m_incl, g2[c:], 0.0)
        vbd = blockdiag(vb[rs, gs])
        nakv = _dot(n_ak.astype(BF16), vbd)
        yield
        mrkv = _dot(m_rk.astype(BF16), vbd)
        yield
        tinv = eye + n_ab
        pw = n_ab
        for _ in range(1, nsq):
            pw = _dot(pw.astype(BF16), blockdiag(pw))
            yield
            tinv = tinv + _dot(pw.astype(BF16), blockdiag(tinv))
            yield
        pre[ci, g] = dict(ar=ar, tinv=tinv.astype(BF16), etot=cv["etot"][:, gs], nakv=nakv, mrkv=mrkv,
                          m_rb=m_rb.astype(BF16),
                          bk=jnp.concatenate([cv["bh"][:, gs], cv["kh"][:, gs]], axis=0).astype(BF16))

    def recurrent_task(g):
        gs = slice(g * gw, (g + 1) * gw)
        s_g = s_ref[g]
        for ci in order:
            while (ci, g) not in pre:
                yield
            q = pre[ci, g]
            ars = _dot_nt(q["ar"], s_g.astype(BF16))
            yield
            u = _dot(q["tinv"], blockdiag(ars[:c] + q["nakv"]))
            yield
            o_parts[ci, g] = ars[c:] + _dot(q["m_rb"], blockdiag(u)) + q["mrkv"]
            yield
            uv = jnp.concatenate([u.astype(BF16), vb[ci * c:(ci + 1) * c, gs]], axis=0)
            s_g = s_g * q["etot"] + jnp.where(bd, _dot_tn(uv, q["bk"]), 0.0)
            yield
        s_ref[g] = s_g

    waiting = [independent_task(ci, g) for ci in order for g in range(ngroup)]
    running = [recurrent_task(g) for g in range(ngroup)]
    lanes = []
    while running or lanes or waiting:
        while waiting and len(lanes) < RW_LANES:
            lanes.append(waiting.pop(0))
        for group in (running, lanes):
            for task in list(group):
                try:
                    next(task)
                except StopIteration:
                    group.remove(task)
    o_chunks = [[o_parts[ci, g] for g in range(ngroup)] for ci in range(nchunk)]
    o = jnp.concatenate([jnp.concatenate(oc, axis=1) for oc in o_chunks], axis=0)
    if reverse:
        out_ref[...] = o.astype(out_ref.dtype)
    else:
        o = o + ob_ref[...].astype(F32)
        inv_n = 1.0 / RW_HEAD_DIM
        mean = headsum(o) * inv_n
        d = o - mean
        var = headsum(d * d) * inv_n
        o = d * lax.rsqrt(var + RW_LN_EPS) * lng_ref[...] + lnb_ref[...]
        a_lr_b = _sigmoid(a0b_ref[...] + _dot(latb, a2b_ref[...]))
        kd_b = k * (1.0 + (a_lr_b - 1.0) * k_a)
        bonus = headsum(r * (kd + kd_b) * rk_ref[...]) * vb.astype(F32)
        gate = _dot(_sigmoid(t_ref[:, 2 * RW_DECAY_RANK:].astype(F32)).astype(BF16), g2_ref[...])
        out_ref[...] = ((o + bonus) * gate).astype(out_ref.dtype)


def _rwkv7(rw, p, segs):
    n = rw.shape[0]
    tb = RW_BLOCK
    nblk = n // tb
    w = RW_WIDTH
    scratch = [pltpu.VMEM((w // RW_GROUP, RW_GROUP, RW_GROUP), F32)]

    def rows(j, width, rev):
        if rev:
            return pl.BlockSpec((tb, width), lambda i: (nblk - 1 - i, j))
        return pl.BlockSpec((tb, width), lambda i: (i, j))

    def full(a):
        return pl.BlockSpec(a.shape, lambda i: (0,) * a.ndim)

    tail_blk = (3 * w) // RW_TAIL
    bwd_params = [p["w0_b"], p["w2h_b"], p["w2l_b"], p["a0_b"], p["a2_b"], p["k_k"], p["k_a"]]
    o_b = pl.pallas_call(
        functools.partial(_rwkv7_body, segs, True, nblk),
        grid=(nblk,),
        in_specs=[rows(0, w, True), rows(1, w, True), rows(2, w, True), rows(tail_blk, RW_TAIL, True)]
        + [full(a) for a in bwd_params],
        out_specs=rows(0, w, True),
        out_shape=jax.ShapeDtypeStruct((n, w), BF16),
        scratch_shapes=scratch,
        compiler_params=_params(),
        name="rwkv7_bwd",
    )(rw, rw, rw, rw, *bwd_params)
    fwd_params = [p["w0_f"], p["w2h_f"], p["w2l_f"], p["a0_f"], p["a2_f"], p["a0_b"], p["a2_b"], p["g2"],
                  p["k_k"], p["k_a"], p["r_k"], p["ln_g"], p["ln_b"]]
    return pl.pallas_call(
        functools.partial(_rwkv7_body, segs, False, nblk),
        grid=(nblk,),
        in_specs=[rows(0, w, False), rows(1, w, False), rows(2, w, False), rows(tail_blk, RW_TAIL, False),
                  rows(0, w, False)] + [full(a) for a in fwd_params],
        out_specs=rows(0, w, False),
        out_shape=jax.ShapeDtypeStruct((n, w), BF16),
        scratch_shapes=scratch,
        compiler_params=_params(),
        name="rwkv7_fwd",
    )(rw, rw, rw, rw, o_b, *fwd_params)


def _store_row_tiles(ref, val):
    rows = val.shape[0]
    for j in range(ROW_TILE):
        ref[pl.ds(j, rows, stride=ROW_TILE), :] = val[:, j * LANES:(j + 1) * LANES]


def _load_row_tiles(ref, rows):
    return jnp.concatenate([ref[pl.ds(j, rows, stride=ROW_TILE), :] for j in range(ROW_TILE)], axis=1)


def _out_router_body(n_a, xa_ref, xb_ref, yh_ref, yr_ref, woa_ref, wob_ref, g_ref, rwh_ref, rwl_ref, rb_ref,
                     x2_ref, h2_ref, idx_ref, gate_ref, rank_ref, cnt_ref, carry_ref):
    tm = xa_ref.shape[0]

    @pl.when(pl.program_id(0) == 0)
    def _():
        carry_ref[...] = jnp.zeros_like(carry_ref)

    x = _pick(pl.program_id(0) * tm < n_a, xa_ref, xb_ref)
    x2 = x + _dot(yh_ref[...], woa_ref[...]) + _dot(yr_ref[...], wob_ref[...])
    x2_ref[...] = x2
    h2 = x2 * lax.rsqrt(jnp.mean(x2 * x2, axis=-1, keepdims=True) + NORM_EPS) * g_ref[...]
    _store_row_tiles(h2_ref, h2)
    hi, lo = _split(h2)
    logits = (_dot_nt(rwh_ref[...], hi) + _dot_nt(rwh_ref[...], lo) + _dot_nt(rwl_ref[...], hi)) + rb_ref[...]
    ids = lax.broadcasted_iota(jnp.int32, logits.shape, 0)
    vals = logits
    tops, picks = [], []
    for _ in range(TOP_K):
        m = jnp.max(vals, axis=0, keepdims=True)
        pick = jnp.min(jnp.where(vals == m, ids, N_EXPERTS), axis=0, keepdims=True)
        vals = jnp.where(ids == pick, -jnp.inf, vals)
        tops.append(m)
        picks.append(pick)
    es = [jnp.exp(t - tops[0]) for t in tops]
    den = es[0] + es[1] + es[2] + es[3]
    idx_ref[...] = jnp.concatenate(picks, axis=0)
    gate_ref[...] = jnp.concatenate([e / den for e in es], axis=0)
    onehots = [(ids == pick).astype(F32) for pick in picks]
    member = (onehots[0] + onehots[1]) + (onehots[2] + onehots[3])
    t0 = lax.broadcasted_iota(jnp.int32, (tm, tm), 0)
    t1 = lax.broadcasted_iota(jnp.int32, (tm, tm), 1)
    earlier = _dot(member.astype(BF16), (t0 < t1).astype(BF16))
    base = carry_ref[:, 0:1] + earlier
    rank_ref[...] = jnp.concatenate([jnp.sum(oh * base, axis=0, keepdims=True) for oh in onehots],
                                    axis=0).astype(jnp.int32)
    carry = carry_ref[...] + jnp.sum(member, axis=1, keepdims=True)
    carry_ref[...] = carry
    cnt_ref[...] = carry


def _out_router(x_a, x_b, y_hg, y_rw, wo_a, wo_b, norm_g, rw_hi, rw_lo, rb):
    d = x_a.shape[1]
    n_a, n_b = x_a.shape[0], x_b.shape[0]
    n = n_a + n_b
    tm = PROJ_TILE

    def full(a):
        return pl.BlockSpec(a.shape, lambda i: (0,) * a.ndim)

    return pl.pallas_call(
        functools.partial(_out_router_body, n_a),
        grid=(n // tm,),
        in_specs=[*_two_array_specs((tm, d), n_a, n_b),
                  pl.BlockSpec((tm, HG_WIDTH), lambda i: (i, 0)),
                  pl.BlockSpec((tm, RW_WIDTH), lambda i: (i, 0)),
                  full(wo_a), full(wo_b), full(norm_g), full(rw_hi), full(rw_lo), full(rb)],
        out_specs=[pl.BlockSpec((tm, d), lambda i: (i, 0)),
                   pl.BlockSpec((tm * ROW_TILE, LANES), lambda i: (i, 0)),
                   pl.BlockSpec((TOP_K, tm), lambda i: (0, i)),
                   pl.BlockSpec((TOP_K, tm), lambda i: (0, i)),
                   pl.BlockSpec((TOP_K, tm), lambda i: (0, i)),
                   pl.BlockSpec((N_EXPERTS, LANES), lambda i: (0, 0))],
        out_shape=[jax.ShapeDtypeStruct((n, d), F32), jax.ShapeDtypeStruct((n * ROW_TILE, LANES), F32),
                   jax.ShapeDtypeStruct((TOP_K, n), jnp.int32), jax.ShapeDtypeStruct((TOP_K, n), F32),
                   jax.ShapeDtypeStruct((TOP_K, n), jnp.int32),
                   jax.ShapeDtypeStruct((N_EXPERTS, LANES), F32)],
        scratch_shapes=[pltpu.VMEM((N_EXPERTS, LANES), F32)],
        compiler_params=_params(),
        name="out_router",
    )(x_a, x_b, y_hg, y_rw, wo_a, wo_b, norm_g, rw_hi, rw_lo, rb)


def _tile_copy(src_hbm, row, dst_vmem, slot, sem):
    return pltpu.make_async_copy(src_hbm.at[pl.ds(pl.multiple_of(row * ROW_TILE, ROW_TILE), ROW_TILE)],
                                 dst_vmem.at[pl.ds(slot * ROW_TILE, ROW_TILE)], sem)


def _moe_body(be_ref, nu_ref, tok_ref, tokn_ref, h_hbm, wgu_ref, bgu_ref, wdn_ref, bdn_ref, out_ref,
              xbuf, sem):
    b = pl.program_id(0)
    nu = nu_ref[0]
    blk = MOE_BLOCK
    cur = lax.rem(b, 2)

    def gather(toks, s):
        for r in range(blk):
            _tile_copy(h_hbm, toks[0, 0, r], xbuf.at[s], r, sem.at[s]).start()

    def drain(s):
        for r in range(blk):
            _tile_copy(h_hbm, 0, xbuf.at[s], r, sem.at[s]).wait()

    @pl.when(jnp.logical_and(b == 0, nu > 0))
    def _():
        gather(tok_ref, 0)

    @pl.when(b < nu)
    def _():
        drain(cur)
        gather(tokn_ref, 1 - cur)
        x = _load_row_tiles(xbuf.at[cur], blk).astype(BF16)
        gu = _dot(x, wgu_ref[0]) + bgu_ref[0]
        glu = jnp.minimum(gu[:, :D_FF], SWIGLU_LIMIT)
        lin = jnp.clip(gu[:, D_FF:], -SWIGLU_LIMIT, SWIGLU_LIMIT)
        act = glu * _sigmoid(SWIGLU_ALPHA * glu) * (lin + 1.0)
        _store_row_tiles(out_ref, _dot(act.astype(BF16), wdn_ref[0]) + bdn_ref[0])

        @pl.when(b == nu - 1)
        def _():
            drain(1 - cur)

    @pl.when(b >= nu)
    def _():
        out_ref[...] = jnp.zeros_like(out_ref)


def _moe(h2_tiles, blk_expert, n_used, row_tok, w_gu, b_gu, w_dn, b_dn):
    d = D_MODEL
    blk = MOE_BLOCK
    nblk = row_tok.shape[0]
    grid_spec = pltpu.PrefetchScalarGridSpec(
        num_scalar_prefetch=2,
        grid=(nblk,),
        in_specs=[
            pl.BlockSpec((1, 1, blk), lambda b, be, nu: (b, 0, 0), memory_space=pltpu.SMEM),
            pl.BlockSpec((1, 1, blk), lambda b, be, nu: (jnp.minimum(b + 1, nblk - 1), 0, 0),
                         memory_space=pltpu.SMEM),
            pl.BlockSpec(memory_space=pl.ANY),
            pl.BlockSpec((1, d, 2 * D_FF), lambda b, be, nu: (be[b], 0, 0)),
            pl.BlockSpec((1, 1, 2 * D_FF), lambda b, be, nu: (be[b], 0, 0)),
            pl.BlockSpec((1, D_FF, d), lambda b, be, nu: (be[b], 0, 0)),
            pl.BlockSpec((1, 1, d), lambda b, be, nu: (be[b], 0, 0)),
        ],
        out_specs=pl.BlockSpec((blk * ROW_TILE, LANES), lambda b, be, nu: (b, 0)),
        scratch_shapes=[pltpu.VMEM((2, blk * ROW_TILE, LANES), F32), pltpu.SemaphoreType.DMA((2,))],
    )
    return pl.pallas_call(
        _moe_body,
        grid_spec=grid_spec,
        out_shape=jax.ShapeDtypeStruct((nblk * blk * ROW_TILE, LANES), F32),
        compiler_params=_params(),
        name="moe_experts",
    )(blk_expert, n_used, row_tok, row_tok, h2_tiles, w_gu, b_gu, w_dn, b_dn)


def _combine_body(n_a, pos_ref, posn_ref, x2_ref, gate_ref, g_ref, yb_hbm, outa_ref, outb_ref, ybuf, sem):
    tm = x2_ref.shape[0]
    i = pl.program_id(0)
    cur = lax.rem(i, 2)

    def gather(pos, s):
        for t in range(tm):
            for k in range(TOP_K):
                _tile_copy(yb_hbm, pos[k, t], ybuf.at[s, k], t, sem.at[s]).start()

    def drain(s):
        for t in range(tm):
            for k in range(TOP_K):
                _tile_copy(yb_hbm, 0, ybuf.at[s, k], t, sem.at[s]).wait()

    @pl.when(i == 0)
    def _():
        gather(pos_ref, 0)

    @pl.when(i + 1 < pl.num_programs(0))
    def _():
        gather(posn_ref, 1 - cur)

    drain(cur)
    rr = lax.broadcasted_iota(jnp.int32, (tm, tm), 0)
    cc = lax.broadcasted_iota(jnp.int32, (tm, tm), 1)
    x = x2_ref[...]
    for k in range(TOP_K):
        gcol = jnp.sum(jnp.where(rr == cc, gate_ref[k:k + 1, :], 0.0), axis=1, keepdims=True)
        x = x + gcol * _load_row_tiles(ybuf.at[cur, k], tm)
    y = x * lax.rsqrt(jnp.mean(x * x, axis=-1, keepdims=True) + NORM_EPS) * g_ref[...]

    @pl.when(i * tm < n_a)
    def _():
        outa_ref[...] = y

    @pl.when(i * tm >= n_a)
    def _():
        outb_ref[...] = y


def _combine(pos_t, gate_t, x2, final_g, yb_tiles, n_a):
    n, d = x2.shape
    tm = COMBINE_TILE
    last = n // tm - 1
    out_a, out_b = _two_array_specs((tm, d), n_a, n - n_a)
    return pl.pallas_call(
        functools.partial(_combine_body, n_a),
        grid=(n // tm,),
        in_specs=[pl.BlockSpec((TOP_K, tm), lambda i: (0, i), memory_space=pltpu.SMEM),
                  pl.BlockSpec((TOP_K, tm), lambda i: (0, jnp.minimum(i + 1, last)), memory_space=pltpu.SMEM),
                  pl.BlockSpec((tm, d), lambda i: (i, 0)),
                  pl.BlockSpec((TOP_K, tm), lambda i: (0, i)),
                  pl.BlockSpec((1, d), lambda i: (0, 0)),
                  pl.BlockSpec(memory_space=pl.ANY)],
        out_specs=[out_a, out_b],
        out_shape=[jax.ShapeDtypeStruct((n_a, d), F32), jax.ShapeDtypeStruct((n - n_a, d), F32)],
        scratch_shapes=[pltpu.VMEM((2, TOP_K, tm * ROW_TILE, LANES), F32), pltpu.SemaphoreType.DMA((2,))],
        compiler_params=_params(),
        name="moe_combine",
    )(pos_t, pos_t, x2, gate_t, final_g, yb_tiles)


def _routing_tables(idx_t, rank_t, counts):
    n = idx_t.shape[1]
    n_assign = n * TOP_K
    blk = MOE_BLOCK
    counts = counts.astype(jnp.int32)
    padded = (counts + blk - 1) // blk * blk
    grp_start = jnp.cumsum(counts) - counts
    pad_end = jnp.cumsum(padded)
    pad_start = pad_end - padded
    onehot = idx_t[..., None] == jnp.arange(N_EXPERTS, dtype=jnp.int32)
    pos_t = rank_t + jnp.sum(jnp.where(onehot, pad_start, 0), axis=-1)
    keys = jnp.sort((idx_t * n + jnp.arange(n, dtype=jnp.int32)[None]).reshape(-1))
    nblk = (n_assign + N_EXPERTS * (blk - 1)) // blk + 1
    blk_start = jnp.arange(nblk, dtype=jnp.int32) * blk
    blk_expert = jnp.minimum(jnp.sum((pad_end[None, :] <= blk_start[:, None]).astype(jnp.int32), axis=1),
                             N_EXPERTS - 1)
    within = (blk_start - pad_start[blk_expert])[:, None] + jnp.arange(blk, dtype=jnp.int32)[None]
    valid = within < counts[blk_expert][:, None]
    src = jnp.clip(grp_start[blk_expert][:, None] + within, 0, n_assign - 1)
    row_tok = jnp.where(valid, keys[src] - blk_expert[:, None] * n, 0)
    n_used = (pad_end[-1] // blk).astype(jnp.int32).reshape(1)
    return blk_expert, n_used, row_tok.reshape(nblk, 1, blk), pos_t


def _pad_rows(w, top, total):
    return jnp.zeros((total, w.shape[1]), w.dtype).at[top:top + w.shape[0]].set(w)


def _encoder(x_a, x_b, segs, norm_mix_g, w_in, hg_lower_bound, hg_norm_g, rw_mu, rw_w0, rw_w2, rw_a0, rw_a2, rw_g2,
             rw_k_k, rw_k_a, rw_r_k, rw_ln_g, rw_ln_b, w_out, norm_ffn_g, router_w, router_b, exp_w_gate_up,
             exp_b_gate_up, exp_w_down, exp_b_down, final_norm_g):
    w_in_b = w_in[0].astype(BF16)
    hg, rw = _in_proj(x_a, x_b, norm_mix_g, w_in_b[:, :HG_COLS], w_in_b[:, HG_COLS:], rw_mu, segs)
    y_hg = _hgrn2(hg, hg_lower_bound, hg_norm_g, segs)
    lat = 2 * RW_DECAY_RANK
    p = {"k_k": rw_k_k, "k_a": rw_k_a, "r_k": rw_r_k, "ln_g": rw_ln_g, "ln_b": rw_ln_b,
         "g2": rw_g2[0].astype(BF16)}
    for d, tag in ((0, "f"), (1, "b")):
        w2 = _pad_rows(rw_w2[0, d], 0, lat)
        p["w2h_" + tag], p["w2l_" + tag] = _split(w2)
        p["a2_" + tag] = _pad_rows(rw_a2[0, d], RW_DECAY_RANK, lat).astype(BF16)
        p["w0_" + tag] = rw_w0[0, d][None]
        p["a0_" + tag] = rw_a0[0, d][None]
    y_rw = _rwkv7(rw, p, segs)
    wo = w_out[0].astype(BF16)
    rw_hi, rw_lo = _split(router_w[0].T)
    x2, h2, idx_t, gate_t, rank_t, counts = _out_router(x_a, x_b, y_hg, y_rw, wo[:HG_WIDTH], wo[HG_WIDTH:],
                                                        norm_ffn_g, rw_hi, rw_lo, router_b[0][:, None])
    blk_expert, n_used, row_tok, pos_t = _routing_tables(idx_t, rank_t, counts[:, 0])
    yb = _moe(h2, blk_expert, n_used, row_tok, exp_w_gate_up[0].astype(BF16),
              exp_b_gate_up[0][:, None, :], exp_w_down[0].astype(BF16), exp_b_down[0][:, None, :])
    return _combine(pos_t, gate_t, x2, final_norm_g[None], yb, x_a.shape[0])


def kernel(x_prompt, x_sample, norm_mix_g, w_in, hg_lower_bound, hg_norm_g, rw_mu, rw_w0, rw_w2, rw_a0, rw_a2,
           rw_g2, rw_k_k, rw_k_a, rw_r_k, rw_ln_g, rw_ln_b, w_out, norm_ffn_g, router_w, router_b, exp_w_gate_up,
           exp_b_gate_up, exp_w_down, exp_b_down, final_norm_g):
    d = x_prompt.shape[-1]
    segs = (x_prompt.shape[:2], x_sample.shape[:2])
    y_p, y_s = _encoder(x_prompt.reshape(-1, d), x_sample.reshape(-1, d), segs, norm_mix_g, w_in, hg_lower_bound,
                        hg_norm_g, rw_mu, rw_w0, rw_w2, rw_a0, rw_a2, rw_g2, rw_k_k, rw_k_a, rw_r_k, rw_ln_g,
                        rw_ln_b, w_out, norm_ffn_g, router_w, router_b, exp_w_gate_up, exp_b_gate_up, exp_w_down,
                        exp_b_down, final_norm_g)
    return y_p.reshape(x_prompt.shape), y_s.reshape(x_sample.shape)
```

```python
import functools

import jax
import jax.numpy as jnp
from jax import lax
from jax.experimental import pallas as pl
from jax.experimental.pallas import tpu as pltpu

F32 = jnp.float32
BF16 = jnp.bfloat16

D_MODEL = 1024
HG_WIDTH = 512
HG_HEAD_DIM = 128
HG_HEADS = HG_WIDTH // HG_HEAD_DIM
RW_WIDTH = 512
RW_HEAD_DIM = 64
RW_DECAY_RANK = 64
RW_AAA_RANK = 64
RW_GATE_RANK = 128
HG_COLS = 5 * HG_WIDTH
RW_COLS = 3 * RW_WIDTH + RW_DECAY_RANK + RW_AAA_RANK + RW_GATE_RANK
RW_TAIL = RW_COLS - 3 * RW_WIDTH
N_EXPERTS = 32
TOP_K = 4
D_FF = D_MODEL
SWIGLU_LIMIT = 7.0
SWIGLU_ALPHA = 1.702
NORM_EPS = 1e-5
RW_LN_EPS = 64e-5

V7X_VMEM_LIMIT_BYTES = 56 * 1024 * 1024
SUBLANES = 8
LANES = 128
ROW_TILE = D_MODEL // LANES
assert ROW_TILE == SUBLANES

PROJ_TILE = 256
ROUTER_TILE = 512
HG_CHUNK = 64
HG_BLOCK = 512
RW_CHUNK = 64
RW_GROUP = 256
RW_BLOCK = 512
RW_LANES = 6
MOE_BLOCK = 256
MOE_FF_CHUNKS = 4
MOE_OUT_SLABS = 4
COMBINE_TILE = 128
EXP_CLAMP = 80.0


def _dot(a, b):
    return jnp.dot(a, b, preferred_element_type=F32)


def _dot_nt(a, b):
    return lax.dot_general(a, b, (((1,), (1,)), ((), ())), preferred_element_type=F32)


def _dot_tn(a, b):
    return lax.dot_general(a, b, (((0,), (0,)), ((), ())), preferred_element_type=F32)


def _split(x):
    hi = x.astype(BF16)
    lo = (x - hi.astype(F32)).astype(BF16)
    return hi, lo


def _dot_exact_lhs(a_bf16, x):
    hi, lo = _split(x)
    return _dot(a_bf16, hi) + _dot(a_bf16, lo)


def _dot3(x, w_hi, w_lo):
    hi, lo = _split(x)
    return _dot(hi, w_hi) + _dot(lo, w_hi) + _dot(hi, w_lo)


def _sigmoid(x):
    return 1.0 / (1.0 + jnp.exp(-x))


def _is_seq_boundary(segs, row):
    base = 0
    hit = row == sum(n * t for n, t in segs)
    for n, t in segs:
        inside = jnp.logical_and(row >= base, row < base + n * t)
        hit = jnp.logical_or(hit, jnp.logical_and(inside, lax.rem(row - base, t) == 0))
        base += n * t
    return hit


def _params(vmem=V7X_VMEM_LIMIT_BYTES):
    return pltpu.CompilerParams(dimension_semantics=("arbitrary",), vmem_limit_bytes=vmem)


def _two_array_specs(block, n_a, n_b, shift=0, unit=None):
    rows = block[0]
    per = (unit or rows) // rows
    nba, nbb = n_a // rows, n_b // rows

    def idx_a(i):
        return (jnp.clip(i * per + shift, 0, nba - 1), 0)

    def idx_b(i):
        return (jnp.clip(i * per + shift - nba, 0, nbb - 1), 0)

    return pl.BlockSpec(block, idx_a), pl.BlockSpec(block, idx_b)


def _pick(in_a, ref_a, ref_b):
    return jnp.where(in_a, ref_a[...], ref_b[...])


def _in_proj_body(segs, xpa_ref, xpb_ref, xa_ref, xb_ref, xna_ref, xnb_ref, g_ref, whg_ref, wrw_ref, mu_ref,
                  hg_ref, rw_ref):
    tm = xa_ref.shape[0]
    n_a = segs[0][0] * segs[0][1]
    row0 = pl.program_id(0) * tm
    at_start = _is_seq_boundary(segs, row0)
    at_end = _is_seq_boundary(segs, row0 + tm)
    g = g_ref[...]

    def nrm(x):
        return x * lax.rsqrt(jnp.mean(x * x, axis=-1, keepdims=True) + NORM_EPS) * g

    h = nrm(_pick(row0 < n_a, xa_ref, xb_ref))
    hp = jnp.where(at_start, 0.0, nrm(_pick(row0 - 1 < n_a, xpa_ref, xpb_ref)))
    hn = jnp.where(at_end, 0.0, nrm(_pick(row0 + tm < n_a, xna_ref, xnb_ref)))
    hb = h.astype(BF16)
    for c in range(HG_COLS // HG_WIDTH):
        cs = slice(c * HG_WIDTH, (c + 1) * HG_WIDTH)
        hg_ref[:, cs] = _dot(hb, whg_ref[:, cs]).astype(BF16)
    hs = jnp.concatenate([hp.astype(BF16), hb, hn.astype(BF16)], axis=0)
    p = _dot(hs, wrw_ref[...])
    rows = tm + 2 * SUBLANES
    cur = p[SUBLANES:SUBLANES + tm]
    prev = pltpu.roll(p, 1, 0)[SUBLANES:SUBLANES + tm]
    nxt = pltpu.roll(p, rows - 1, 0)[SUBLANES:SUBLANES + tm]
    mu = mu_ref[...]
    rw_ref[...] = (cur + mu * (0.5 * (prev + nxt) - cur)).astype(BF16)


def _in_proj(x_a, x_b, norm_g, w_hg, w_rw, mu, segs):
    d = x_a.shape[1]
    n_a, n_b = x_a.shape[0], x_b.shape[0]
    n = n_a + n_b
    tm = PROJ_TILE
    return pl.pallas_call(
        functools.partial(_in_proj_body, segs),
        grid=(n // tm,),
        in_specs=[
            *_two_array_specs((SUBLANES, d), n_a, n_b, shift=-1, unit=tm),
            *_two_array_specs((tm, d), n_a, n_b),
            *_two_array_specs((SUBLANES, d), n_a, n_b, shift=tm // SUBLANES, unit=tm),
            pl.BlockSpec((1, d), lambda i: (0, 0)),
            pl.BlockSpec((d, HG_COLS), lambda i: (0, 0)),
            pl.BlockSpec((d, RW_COLS), lambda i: (0, 0)),
            pl.BlockSpec((1, RW_COLS), lambda i: (0, 0)),
        ],
        out_specs=[
            pl.BlockSpec((tm, HG_COLS), lambda i: (i, 0)),
            pl.BlockSpec((tm, RW_COLS), lambda i: (i, 0)),
        ],
        out_shape=[jax.ShapeDtypeStruct((n, HG_COLS), BF16), jax.ShapeDtypeStruct((n, RW_COLS), BF16)],
        compiler_params=_params(),
        name="in_proj",
    )(x_a, x_b, x_a, x_b, x_a, x_b, norm_g, w_hg, w_rw, mu)


def _tri(c, reverse):
    t = lax.broadcasted_iota(jnp.int32, (c, c), 0)
    s = lax.broadcasted_iota(jnp.int32, (c, c), 1)
    return (s >= t) if reverse else (s <= t)


def _hgrn2_body(segs, reverse, nblk, *refs):
    if reverse:
        q_ref, f_ref, i_ref, lb_ref, out_ref, st_ref = refs
    else:
        q_ref, f_ref, i_ref, g_ref, ob_ref, lb_ref, ng_ref, out_ref, st_ref = refs
    tb = q_ref.shape[0]
    c = HG_CHUNK
    step = pl.program_id(0)
    blk = (nblk - 1 - step) if reverse else step
    edge = (blk + 1) * tb if reverse else blk * tb

    @pl.when(_is_seq_boundary(segs, edge))
    def _():
        st_ref[...] = jnp.zeros_like(st_ref)

    lbp = lb_ref[...]
    e0 = jnp.exp(lbp[0:1] - jnp.maximum(lbp[0:1], lbp[1:2]))
    e1 = jnp.exp(lbp[1:2] - jnp.maximum(lbp[0:1], lbp[1:2]))
    lb = e0 / (e0 + e1)
    tri = _tri(c, reverse)
    tri_b = tri.astype(BF16)
    mid = c // 2 if reverse else c // 2 - 1
    last = 0 if reverse else c - 1
    nchunk = tb // c
    order = list(range(nchunk - 1, -1, -1) if reverse else range(nchunk))
    heads = [slice(h * HG_HEAD_DIM, (h + 1) * HG_HEAD_DIM) for h in range(HG_HEADS)]
    gates = {ci: lb + (1.0 - lb) * _sigmoid(f_ref[ci * c:(ci + 1) * c, :].astype(F32)) for ci in order}
    cums = {ci: _dot_exact_lhs(tri_b, jnp.log(gates[ci])) for ci in order}
    cv = {}
    for ci in order:
        rs = slice(ci * c, (ci + 1) * c)
        q = q_ref[rs, :].astype(F32)
        kx = 1.0 - gates[ci]
        cum = cums[ci]
        rho = cum[mid:mid + 1]
        tot = cum[last:last + 1]
        qs = q * _sigmoid(q)
        qh = (qs * jnp.exp(jnp.minimum(cum - rho, EXP_CLAMP))).astype(BF16)
        kh = (kx * jnp.exp(jnp.minimum(rho - cum, EXP_CLAMP))).astype(BF16)
        ke = (kx * jnp.exp(tot - cum)).astype(BF16)
        vv = i_ref[rs, :]
        cv[ci] = dict(qt=(qs * jnp.exp(cum)).astype(BF16), etot=jnp.exp(tot), vv=vv,
                      sc=[jnp.where(tri, _dot_nt(qh[:, hs], kh[:, hs]), 0.0).astype(BF16) for hs in heads],
                      kv=[_dot_tn(vv[:, hs], ke[:, hs]) for hs in heads])
    states = [st_ref[h] for h in range(HG_HEADS)]
    for ci in order:
        rs = slice(ci * c, (ci + 1) * c)
        q = cv[ci]
        outs = []
        for h, hs in enumerate(heads):
            outs.append(_dot(q["sc"][h], q["vv"][:, hs]) + _dot_nt(q["qt"][:, hs], states[h].astype(BF16)))
            states[h] = states[h] * q["etot"][:, hs] + q["kv"][h]
        if ci == order[-1]:
            for h in range(HG_HEADS):
                st_ref[h] = states[h]
        if reverse:
            for h in range(HG_HEADS):
                out_ref[rs, h * HG_HEAD_DIM:(h + 1) * HG_HEAD_DIM] = outs[h].astype(out_ref.dtype)
        else:
            ng = ng_ref[...]
            gg = g_ref[rs, :].astype(F32)
            gate = gg * _sigmoid(gg)
            for h in range(HG_HEADS):
                hs = slice(h * HG_HEAD_DIM, (h + 1) * HG_HEAD_DIM)
                o = outs[h] + ob_ref[rs, hs].astype(F32)
                o = o * lax.rsqrt(jnp.mean(o * o, axis=-1, keepdims=True) + NORM_EPS) * ng[:, hs]
                out_ref[rs, hs] = (o * gate[:, hs]).astype(out_ref.dtype)


def _hgrn2(hg, lower_bound, norm_g, segs):
    n = hg.shape[0]
    tb = HG_BLOCK
    nblk = n // tb
    w = HG_WIDTH
    scratch = [pltpu.VMEM((HG_HEADS, HG_HEAD_DIM, HG_HEAD_DIM), F32)]

    def col(j, rev):
        if rev:
            return pl.BlockSpec((tb, w), lambda i: (nblk - 1 - i, j))
        return pl.BlockSpec((tb, w), lambda i: (i, j))

    o_b = pl.pallas_call(
        functools.partial(_hgrn2_body, segs, True, nblk),
        grid=(nblk,),
        in_specs=[col(0, True), col(2, True), col(3, True), pl.BlockSpec((2, w), lambda i: (0, 0))],
        out_specs=col(0, True),
        out_shape=jax.ShapeDtypeStruct((n, w), BF16),
        scratch_shapes=scratch,
        compiler_params=_params(),
        name="hgrn2_bwd",
    )(hg, hg, hg, lower_bound)
    return pl.pallas_call(
        functools.partial(_hgrn2_body, segs, False, nblk),
        grid=(nblk,),
        in_specs=[col(0, False), col(1, False), col(3, False), col(4, False), col(0, False),
                  pl.BlockSpec((2, w), lambda i: (0, 0)), pl.BlockSpec((1, w), lambda i: (0, 0))],
        out_specs=col(0, False),
        out_shape=jax.ShapeDtypeStruct((n, w), BF16),
        scratch_shapes=scratch,
        compiler_params=_params(),
        name="hgrn2_fwd",
    )(hg, hg, hg, hg, o_b, lower_bound, norm_g)


def _block_diag_mask(n, blk):
    shift = blk.bit_length() - 1
    assert 1 << shift == blk
    r = lax.broadcasted_iota(jnp.int32, (n, n), 0) >> shift
    c = lax.broadcasted_iota(jnp.int32, (n, n), 1) >> shift
    return r == c


def _rwkv7_body(segs, reverse, nblk, *refs):
    if reverse:
        (r_ref, k_ref, v_ref, t_ref, w0_ref, w2h_ref, w2l_ref, a0_ref, a2_ref, kk_ref, ka_ref,
         out_ref, s_ref) = refs
    else:
        (r_ref, k_ref, v_ref, t_ref, ob_ref, w0_ref, w2h_ref, w2l_ref, a0_ref, a2_ref, a0b_ref, a2b_ref,
         g2_ref, kk_ref, ka_ref, rk_ref, lng_ref, lnb_ref, out_ref, s_ref) = refs
    tb = r_ref.shape[0]
    c = RW_CHUNK
    gw = RW_GROUP
    ngroup = RW_WIDTH // gw
    hpg = gw // RW_HEAD_DIM
    step = pl.program_id(0)
    blk = (nblk - 1 - step) if reverse else step
    edge = (blk + 1) * tb if reverse else blk * tb

    @pl.when(_is_seq_boundary(segs, edge))
    def _():
        s_ref[...] = jnp.zeros_like(s_ref)

    tri = _tri(c, reverse)
    tri_b = tri.astype(BF16)
    bd = _block_diag_mask(gw, RW_HEAD_DIM)
    bd_b = bd.astype(BF16)
    tt = lax.broadcasted_iota(jnp.int32, (c, gw), 0)
    ss = lax.broadcasted_iota(jnp.int32, (c, gw), 1) & (c - 1)
    m_strict = (ss > tt) if reverse else (ss < tt)
    m_incl = (ss >= tt) if reverse else (ss <= tt)
    eye = (ss == tt).astype(F32)
    last = 0 if reverse else c - 1

    def blockdiag(x):
        return jnp.concatenate([x.astype(BF16)] * hpg, axis=0) * bd_b

    def headsum(x):
        rows = x.shape[0]
        hi, lo = _split(jnp.concatenate([x[:, g * gw:(g + 1) * gw] for g in range(ngroup)], axis=0))
        s = _dot(jnp.concatenate([hi, lo], axis=0), bd_b)
        s = s[:ngroup * rows] + s[ngroup * rows:]
        return jnp.concatenate([s[g * rows:(g + 1) * rows] for g in range(ngroup)], axis=1)

    k_a = ka_ref[...]
    r = r_ref[...].astype(F32)
    k = k_ref[...].astype(F32)
    vb = v_ref[...]
    lat = t_ref[:, 0:2 * RW_DECAY_RANK].astype(F32)
    latb = lat.astype(BF16)
    wpre = w0_ref[...] + _dot3(jnp.tanh(lat), w2h_ref[...], w2l_ref[...])
    sp = jnp.maximum(-wpre, 0.0) + jnp.log(1.0 + jnp.exp(-jnp.abs(wpre)))
    lw = -jnp.exp(-sp - 0.5)
    a_lr = _sigmoid(a0_ref[...] + _dot(latb, a2_ref[...]))
    kkr = k * kk_ref[...]
    kkn = kkr / jnp.maximum(jnp.sqrt(headsum(kkr * kkr)), 1e-12)
    kd = k * (1.0 + (a_lr - 1.0) * k_a)
    bvec = kkn * a_lr
    nchunk = tb // c
    nsq = c.bit_length() - 1

    order = list(range(nchunk - 1, -1, -1) if reverse else range(nchunk))
    chunk_vals = {}
    pre = {}
    o_parts = {}

    def chunk_prep(ci):
        rs = slice(ci * c, (ci + 1) * c)
        cum = _dot_exact_lhs(tri_b, lw[rs])
        tot = cum[last:last + 1]
        ecn = jnp.exp(-cum)
        eend = jnp.exp(tot - cum)
        chunk_vals[ci] = dict(rt=r[rs] * jnp.exp(cum), at=-kkn[rs] * jnp.exp(cum - lw[rs]),
                              bt=bvec[rs] * ecn, kt=kd[rs] * ecn, bh=bvec[rs] * eend, kh=kd[rs] * eend,
                              etot=jnp.exp(tot))

    def independent_task(ci, g):
        if ci not in chunk_vals:
            chunk_prep(ci)
            yield
        cv = chunk_vals[ci]
        rs = slice(ci * c, (ci + 1) * c)
        gs = slice(g * gw, (g + 1) * gw)
        ar = jnp.concatenate([cv["at"][:, gs], cv["rt"][:, gs]], axis=0).astype(BF16)
        g1 = _dot_nt(ar, blockdiag(cv["bt"][:, gs]))
        yield
        g2 = _dot_nt(ar, blockdiag(cv["kt"][:, gs]))
        yield
        n_ab = jnp.where(m_strict, g1[:c], 0.0)
        n_ak = jnp.where(m_strict, g2[:c], 0.0)
        m_rb = jnp.where(m_incl, g1[c:], 0.0)
        m_rk = jnp.where(m_incl, g2[c:], 0.0)
        vbd = blockdiag(vb[rs, gs])
        nakv = _dot(n_ak.astype(BF16), vbd)
        yield
        mrkv = _dot(m_rk.astype(BF16), vbd)
        yield
        tinv = eye + n_ab
        pw = n_ab
        for _ in range(1, nsq):
            pw = _dot(pw.astype(BF16), blockdiag(pw))
            yield
            tinv = tinv + _dot(pw.astype(BF16), blockdiag(tinv))
            yield
        pre[ci, g] = dict(ar=ar, tinv=tinv.astype(BF16), etot=cv["etot"][:, gs], nakv=nakv, mrkv=mrkv,
                          m_rb=m_rb.astype(BF16),
                          bk=jnp.concatenate([cv["bh"][:, gs], cv["kh"][:, gs]], axis=0).astype(BF16))

    def recurrent_task(g):
        gs = slice(g * gw, (g + 1) * gw)
        s_g = s_ref[g]
        for ci in order:
            while (ci, g) not in pre:
                yield
            q = pre[ci, g]
            ars = _dot_nt(q["ar"], s_g.astype(BF16))
            yield
            u = _dot(q["tinv"], blockdiag(ars[:c] + q["nakv"]))
            yield
            o_parts[ci, g] = ars[c:] + _dot(q["m_rb"], blockdiag(u)) + q["mrkv"]
            yield
            uv = jnp.concatenate([u.astype(BF16), vb[ci * c:(ci + 1) * c, gs]], axis=0)
            s_g = s_g * q["etot"] + jnp.where(bd, _dot_tn(uv, q["bk"]), 0.0)
            yield
        s_ref[g] = s_g

    waiting = [independent_task(ci, g) for ci in order for g in range(ngroup)]
    running = [recurrent_task(g) for g in range(ngroup)]
    lanes = []
    while running or lanes or waiting:
        while waiting and len(lanes) < RW_LANES:
            lanes.append(waiting.pop(0))
        for group in (running, lanes):
            for task in list(group):
                try:
                    next(task)
                except StopIteration:
                    group.remove(task)
    o_chunks = [[o_parts[ci, g] for g in range(ngroup)] for ci in range(nchunk)]
    o = jnp.concatenate([jnp.concatenate(oc, axis=1) for oc in o_chunks], axis=0)
    if reverse:
        out_ref[...] = o.astype(out_ref.dtype)
    else:
        o = o + ob_ref[...].astype(F32)
        inv_n = 1.0 / RW_HEAD_DIM
        mean = headsum(o) * inv_n
        d = o - mean
        var = headsum(d * d) * inv_n
        o = d * lax.rsqrt(var + RW_LN_EPS) * lng_ref[...] + lnb_ref[...]
        a_lr_b = _sigmoid(a0b_ref[...] + _dot(latb, a2b_ref[...]))
        kd_b = k * (1.0 + (a_lr_b - 1.0) * k_a)
        bonus = headsum(r * (kd + kd_b) * rk_ref[...]) * vb.astype(F32)
        gate = _dot(_sigmoid(t_ref[:, 2 * RW_DECAY_RANK:].astype(F32)).astype(BF16), g2_ref[...])
        out_ref[...] = ((o + bonus) * gate).astype(out_ref.dtype)


def _rwkv7(rw, p, segs):
    n = rw.shape[0]
    tb = RW_BLOCK
    nblk = n // tb
    w = RW_WIDTH
    scratch = [pltpu.VMEM((w // RW_GROUP, RW_GROUP, RW_GROUP), F32)]

    def rows(j, width, rev):
        if rev:
            return pl.BlockSpec((tb, width), lambda i: (nblk - 1 - i, j))
        return pl.BlockSpec((tb, width), lambda i: (i, j))

    def full(a):
        return pl.BlockSpec(a.shape, lambda i: (0,) * a.ndim)

    tail_blk = (3 * w) // RW_TAIL
    bwd_params = [p["w0_b"], p["w2h_b"], p["w2l_b"], p["a0_b"], p["a2_b"], p["k_k"], p["k_a"]]
    o_b = pl.pallas_call(
        functools.partial(_rwkv7_body, segs, True, nblk),
        grid=(nblk,),
        in_specs=[rows(0, w, True), rows(1, w, True), rows(2, w, True), rows(tail_blk, RW_TAIL, True)]
        + [full(a) for a in bwd_params],
        out_specs=rows(0, w, True),
        out_shape=jax.ShapeDtypeStruct((n, w), BF16),
        scratch_shapes=scratch,
        compiler_params=_params(),
        name="rwkv7_bwd",
    )(rw, rw, rw, rw, *bwd_params)
    fwd_params = [p["w0_f"], p["w2h_f"], p["w2l_f"], p["a0_f"], p["a2_f"], p["a0_b"], p["a2_b"], p["g2"],
                  p["k_k"], p["k_a"], p["r_k"], p["ln_g"], p["ln_b"]]
    return pl.pallas_call(
        functools.partial(_rwkv7_body, segs, False, nblk),
        grid=(nblk,),
        in_specs=[rows(0, w, False), rows(1, w, False), rows(2, w, False), rows(tail_blk, RW_TAIL, False),
                  rows(0, w, False)] + [full(a) for a in fwd_params],
        out_specs=rows(0, w, False),
        out_shape=jax.ShapeDtypeStruct((n, w), BF16),
        scratch_shapes=scratch,
        compiler_params=_params(),
        name="rwkv7_fwd",
    )(rw, rw, rw, rw, o_b, *fwd_params)


def _store_row_tiles(ref, val):
    rows = val.shape[0]
    for j in range(ROW_TILE):
        ref[pl.ds(j, rows, stride=ROW_TILE), :] = val[:, j * LANES:(j + 1) * LANES]


def _load_row_tiles(ref, rows):
    return jnp.concatenate([ref[pl.ds(j, rows, stride=ROW_TILE), :] for j in range(ROW_TILE)], axis=1)


def _out_router_body(n_a, xa_ref, xb_ref, yh_ref, yr_ref, woa_ref, wob_ref, g_ref, rwh_ref, rwl_ref, rb_ref,
                     x2_ref, h2_ref, idx_ref, gate_ref, rank_ref, cnt_ref, carry_ref):
    tm = xa_ref.shape[0]

    @pl.when(pl.program_id(0) == 0)
    def _():
        carry_ref[...] = jnp.zeros_like(carry_ref)

    x = _pick(pl.program_id(0) * tm < n_a, xa_ref, xb_ref)
    x2 = x + _dot(yh_ref[...], woa_ref[...]) + _dot(yr_ref[...], wob_ref[...])
    x2_ref[...] = x2
    h2 = x2 * lax.rsqrt(jnp.mean(x2 * x2, axis=-1, keepdims=True) + NORM_EPS) * g_ref[...]
    _store_row_tiles(h2_ref, h2)
    hi, lo = _split(h2)
    logits = (_dot_nt(rwh_ref[...], hi) + _dot_nt(rwh_ref[...], lo) + _dot_nt(rwl_ref[...], hi)) + rb_ref[...]
    ids = lax.broadcasted_iota(jnp.int32, logits.shape, 0)
    vals = logits
    tops, picks = [], []
    for _ in range(TOP_K):
        m = jnp.max(vals, axis=0, keepdims=True)
        pick = jnp.min(jnp.where(vals == m, ids, N_EXPERTS), axis=0, keepdims=True)
        vals = jnp.where(ids == pick, -jnp.inf, vals)
        tops.append(m)
        picks.append(pick)
    es = [jnp.exp(t - tops[0]) for t in tops]
    den = es[0] + es[1] + es[2] + es[3]
    idx_ref[...] = jnp.concatenate(picks, axis=0)
    gate_ref[...] = jnp.concatenate([e / den for e in es], axis=0)
    onehots = [(ids == pick).astype(F32) for pick in picks]
    member = (onehots[0] + onehots[1]) + (onehots[2] + onehots[3])
    t0 = lax.broadcasted_iota(jnp.int32, (tm, tm), 0)
    t1 = lax.broadcasted_iota(jnp.int32, (tm, tm), 1)
    earlier = _dot(member.astype(BF16), (t0 < t1).astype(BF16))
    base = carry_ref[:, 0:1] + earlier
    rank_ref[...] = jnp.concatenate([jnp.sum(oh * base, axis=0, keepdims=True) for oh in onehots],
                                    axis=0).astype(jnp.int32)
    carry = carry_ref[...] + jnp.sum(member, axis=1, keepdims=True)
    carry_ref[...] = carry
    cnt_ref[...] = carry


def _out_router(x_a, x_b, y_hg, y_rw, wo_a, wo_b, norm_g, rw_hi, rw_lo, rb):
    d = x_a.shape[1]
    n_a, n_b = x_a.shape[0], x_b.shape[0]
    n = n_a + n_b
    tm = ROUTER_TILE

    def full(a):
        return pl.BlockSpec(a.shape, lambda i: (0,) * a.ndim)

    return pl.pallas_call(
        functools.partial(_out_router_body, n_a),
        grid=(n // tm,),
        in_specs=[*_two_array_specs((tm, d), n_a, n_b),
                  pl.BlockSpec((tm, HG_WIDTH), lambda i: (i, 0)),
                  pl.BlockSpec((tm, RW_WIDTH), lambda i: (i, 0)),
                  full(wo_a), full(wo_b), full(norm_g), full(rw_hi), full(rw_lo), full(rb)],
        out_specs=[pl.BlockSpec((tm, d), lambda i: (i, 0)),
                   pl.BlockSpec((tm * ROW_TILE, LANES), lambda i: (i, 0)),
                   pl.BlockSpec((TOP_K, tm), lambda i: (0, i)),
                   pl.BlockSpec((TOP_K, tm), lambda i: (0, i)),
                   pl.BlockSpec((TOP_K, tm), lambda i: (0, i)),
                   pl.BlockSpec((N_EXPERTS, LANES), lambda i: (0, 0))],
        out_shape=[jax.ShapeDtypeStruct((n, d), F32), jax.ShapeDtypeStruct((n * ROW_TILE, LANES), F32),
                   jax.ShapeDtypeStruct((TOP_K, n), jnp.int32), jax.ShapeDtypeStruct((TOP_K, n), F32),
                   jax.ShapeDtypeStruct((TOP_K, n), jnp.int32),
                   jax.ShapeDtypeStruct((N_EXPERTS, LANES), F32)],
        scratch_shapes=[pltpu.VMEM((N_EXPERTS, LANES), F32)],
        compiler_params=_params(),
        name="out_router",
    )(x_a, x_b, y_hg, y_rw, wo_a, wo_b, norm_g, rw_hi, rw_lo, rb)


def _tile_copy(src_hbm, row, dst_vmem, slot, sem):
    return pltpu.make_async_copy(src_hbm.at[pl.ds(pl.multiple_of(row * ROW_TILE, ROW_TILE), ROW_TILE)],
                                 dst_vmem.at[pl.ds(slot * ROW_TILE, ROW_TILE)], sem)


def _moe_body(be_ref, nu_ref, tok_ref, tokn_ref, h_hbm, wgu_ref, bgu_ref, wdn_ref, bdn_ref, out_ref,
              xbuf, sem):
    b = pl.program_id(0)
    nu = nu_ref[0]
    blk = MOE_BLOCK
    cur = lax.rem(b, 2)

    def gather(toks, s, part=0, parts=1):
        for r in range(part * blk // parts, (part + 1) * blk // parts):
            _tile_copy(h_hbm, toks[0, 0, r], xbuf.at[s], r, sem.at[s]).start()

    def drain(s):
        for r in range(blk):
            _tile_copy(h_hbm, 0, xbuf.at[s], r, sem.at[s]).wait()

    @pl.when(jnp.logical_and(b == 0, nu > 0))
    def _():
        gather(tok_ref, 0)

    @pl.when(b < nu)
    def _():
        drain(cur)
        x = _load_row_tiles(xbuf.at[cur], blk).astype(BF16)
        cw = D_FF // MOE_FF_CHUNKS

        def gate_up(j):
            a = slice(j * cw, (j + 1) * cw)
            b = slice(D_FF + j * cw, D_FF + (j + 1) * cw)
            return (_dot(x, wgu_ref[0, :, a]) + bgu_ref[0, :, a], _dot(x, wgu_ref[0, :, b]) + bgu_ref[0, :, b])

        acts = []
        for j in range(MOE_FF_CHUNKS):
            gu = gate_up(j)
            glu = jnp.minimum(gu[0], SWIGLU_LIMIT)
            lin = jnp.clip(gu[1], -SWIGLU_LIMIT, SWIGLU_LIMIT)
            acts.append((glu * _sigmoid(SWIGLU_ALPHA * glu) * (lin + 1.0)).astype(BF16))
        act = jnp.concatenate(acts, axis=1)
        slab = D_MODEL // MOE_OUT_SLABS
        per = slab // LANES
        for s in range(MOE_OUT_SLABS):
            cols = slice(s * slab, (s + 1) * slab)
            y = _dot(act, wdn_ref[0, :, cols]) + bdn_ref[0, :, cols]
            gather(tokn_ref, 1 - cur, s, MOE_OUT_SLABS)
            for j in range(per):
                out_ref[pl.ds(s * per + j, blk, stride=ROW_TILE), :] = y[:, j * LANES:(j + 1) * LANES]

        @pl.when(b == nu - 1)
        def _():
            drain(1 - cur)

    @pl.when(b >= nu)
    def _():
        out_ref[...] = jnp.zeros_like(out_ref)


def _moe(h2_tiles, blk_expert, n_used, row_tok, w_gu, b_gu, w_dn, b_dn):
    d = D_MODEL
    blk = MOE_BLOCK
    nblk = row_tok.shape[0]
    grid_spec = pltpu.PrefetchScalarGridSpec(
        num_scalar_prefetch=2,
        grid=(nblk,),
        in_specs=[
            pl.BlockSpec((1, 1, blk), lambda b, be, nu: (b, 0, 0), memory_space=pltpu.SMEM),
            pl.BlockSpec((1, 1, blk), lambda b, be, nu: (jnp.minimum(b + 1, nblk - 1), 0, 0),
                         memory_space=pltpu.SMEM),
            pl.BlockSpec(memory_space=pl.ANY),
            pl.BlockSpec((1, d, 2 * D_FF), lambda b, be, nu: (be[b], 0, 0)),
            pl.BlockSpec((1, 1, 2 * D_FF), lambda b, be, nu: (be[b], 0, 0)),
            pl.BlockSpec((1, D_FF, d), lambda b, be, nu: (be[b], 0, 0)),
            pl.BlockSpec((1, 1, d), lambda b, be, nu: (be[b], 0, 0)),
        ],
        out_specs=pl.BlockSpec((blk * ROW_TILE, LANES), lambda b, be, nu: (b, 0)),
        scratch_shapes=[pltpu.VMEM((2, blk * ROW_TILE, LANES), F32), pltpu.SemaphoreType.DMA((2,))],
    )
    return pl.pallas_call(
        _moe_body,
        grid_spec=grid_spec,
        out_shape=jax.ShapeDtypeStruct((nblk * blk * ROW_TILE, LANES), F32),
        compiler_params=_params(),
        name="moe_experts",
    )(blk_expert, n_used, row_tok, row_tok, h2_tiles, w_gu, b_gu, w_dn, b_dn)


def _combine_body(n_a, pos_ref, posn_ref, x2_ref, gate_ref, g_ref, yb_hbm, outa_ref, outb_ref, ybuf, sem):
    tm = x2_ref.shape[0]
    i = pl.program_id(0)
    cur = lax.rem(i, 2)

    def gather(pos, s):
        for t in range(tm):
            for k in range(TOP_K):
                _tile_copy(yb_hbm, pos[k, t], ybuf.at[s, k], t, sem.at[s]).start(priority=k % 2)

    def drain(s):
        for t in range(tm):
            for k in range(TOP_K):
                _tile_copy(yb_hbm, 0, ybuf.at[s, k], t, sem.at[s]).wait()

    @pl.when(i == 0)
    def _():
        gather(pos_ref, 0)

    @pl.when(i + 1 < pl.num_programs(0))
    def _():
        gather(posn_ref, 1 - cur)

    drain(cur)
    rr = lax.broadcasted_iota(jnp.int32, (tm, tm), 0)
    cc = lax.broadcasted_iota(jnp.int32, (tm, tm), 1)
    x = x2_ref[...]
    for k in range(TOP_K):
        gcol = jnp.sum(jnp.where(rr == cc, gate_ref[k:k + 1, :], 0.0), axis=1, keepdims=True)
        x = x + gcol * _load_row_tiles(ybuf.at[cur, k], tm)
    y = x * lax.rsqrt(jnp.mean(x * x, axis=-1, keepdims=True) + NORM_EPS) * g_ref[...]

    @pl.when(i * tm < n_a)
    def _():
        outa_ref[...] = y

    @pl.when(i * tm >= n_a)
    def _():
        outb_ref[...] = y


def _combine(pos_t, gate_t, x2, final_g, yb_tiles, n_a):
    n, d = x2.shape
    tm = COMBINE_TILE
    last = n // tm - 1
    out_a, out_b = _two_array_specs((tm, d), n_a, n - n_a)
    return pl.pallas_call(
        functools.partial(_combine_body, n_a),
        grid=(n // tm,),
        in_specs=[pl.BlockSpec((TOP_K, tm), lambda i: (0, i), memory_space=pltpu.SMEM),
                  pl.BlockSpec((TOP_K, tm), lambda i: (0, jnp.minimum(i + 1, last)), memory_space=pltpu.SMEM),
                  pl.BlockSpec((tm, d), lambda i: (i, 0)),
                  pl.BlockSpec((TOP_K, tm), lambda i: (0, i)),
                  pl.BlockSpec((1, d), lambda i: (0, 0)),
                  pl.BlockSpec(memory_space=pl.ANY)],
        out_specs=[out_a, out_b],
        out_shape=[jax.ShapeDtypeStruct((n_a, d), F32), jax.ShapeDtypeStruct((n - n_a, d), F32)],
        scratch_shapes=[pltpu.VMEM((2, TOP_K, tm * ROW_TILE, LANES), F32), pltpu.SemaphoreType.DMA((2,))],
        compiler_params=_params(),
        name="moe_combine",
    )(pos_t, pos_t, x2, gate_t, final_g, yb_tiles)


def _routing_tables(idx_t, rank_t, counts):
    n = idx_t.shape[1]
    n_assign = n * TOP_K
    blk = MOE_BLOCK
    counts = counts.astype(jnp.int32)
    padded = (counts + blk - 1) // blk * blk
    grp_start = jnp.cumsum(counts) - counts
    pad_end = jnp.cumsum(padded)
    pad_start = pad_end - padded
    onehot = idx_t[..., None] == jnp.arange(N_EXPERTS, dtype=jnp.int32)
    pos_t = rank_t + jnp.sum(jnp.where(onehot, pad_start, 0), axis=-1)
    keys = jnp.sort((idx_t * n + jnp.arange(n, dtype=jnp.int32)[None]).reshape(-1))
    nblk = (n_assign + N_EXPERTS * (blk - 1)) // blk + 1
    blk_start = jnp.arange(nblk, dtype=jnp.int32) * blk
    blk_expert = jnp.minimum(jnp.sum((pad_end[None, :] <= blk_start[:, None]).astype(jnp.int32), axis=1),
                             N_EXPERTS - 1)
    within = (blk_start - pad_start[blk_expert])[:, None] + jnp.arange(blk, dtype=jnp.int32)[None]
    valid = within < counts[blk_expert][:, None]
    src = jnp.clip(grp_start[blk_expert][:, None] + within, 0, n_assign - 1)
    row_tok = jnp.where(valid, keys[src] - blk_expert[:, None] * n, 0)
    n_used = (pad_end[-1] // blk).astype(jnp.int32).reshape(1)
    return blk_expert, n_used, row_tok.reshape(nblk, 1, blk), pos_t


def _pad_rows(w, top, total):
    return jnp.zeros((total, w.shape[1]), w.dtype).at[top:top + w.shape[0]].set(w)


def _encoder(x_a, x_b, segs, norm_mix_g, w_in, hg_lower_bound, hg_norm_g, rw_mu, rw_w0, rw_w2, rw_a0, rw_a2, rw_g2,
             rw_k_k, rw_k_a, rw_r_k, rw_ln_g, rw_ln_b, w_out, norm_ffn_g, router_w, router_b, exp_w_gate_up,
             exp_b_gate_up, exp_w_down, exp_b_down, final_norm_g):
    w_in_b = w_in[0].astype(BF16)
    hg, rw = _in_proj(x_a, x_b, norm_mix_g, w_in_b[:, :HG_COLS], w_in_b[:, HG_COLS:], rw_mu, segs)
    y_hg = _hgrn2(hg, hg_lower_bound, hg_norm_g, segs)
    lat = 2 * RW_DECAY_RANK
    p = {"k_k": rw_k_k, "k_a": rw_k_a, "r_k": rw_r_k, "ln_g": rw_ln_g, "ln_b": rw_ln_b,
         "g2": rw_g2[0].astype(BF16)}
    for d, tag in ((0, "f"), (1, "b")):
        w2 = _pad_rows(rw_w2[0, d], 0, lat)
        p["w2h_" + tag], p["w2l_" + tag] = _split(w2)
        p["a2_" + tag] = _pad_rows(rw_a2[0, d], RW_DECAY_RANK, lat).astype(BF16)
        p["w0_" + tag] = rw_w0[0, d][None]
        p["a0_" + tag] = rw_a0[0, d][None]
    y_rw = _rwkv7(rw, p, segs)
    wo = w_out[0].astype(BF16)
    rw_hi, rw_lo = _split(router_w[0].T)
    x2, h2, idx_t, gate_t, rank_t, counts = _out_router(x_a, x_b, y_hg, y_rw, wo[:HG_WIDTH], wo[HG_WIDTH:],
                                                        norm_ffn_g, rw_hi, rw_lo, router_b[0][:, None])
    blk_expert, n_used, row_tok, pos_t = _routing_tables(idx_t, rank_t, counts[:, 0])
    yb = _moe(h2, blk_expert, n_used, row_tok, exp_w_gate_up[0].astype(BF16),
              exp_b_gate_up[0][:, None, :], exp_w_down[0].astype(BF16), exp_b_down[0][:, None, :])
    return _combine(pos_t, gate_t, x2, final_norm_g[None], yb, x_a.shape[0])


def kernel(x_prompt, x_sample, norm_mix_g, w_in, hg_lower_bound, hg_norm_g, rw_mu, rw_w0, rw_w2, rw_a0, rw_a2,
           rw_g2, rw_k_k, rw_k_a, rw_r_k, rw_ln_g, rw_ln_b, w_out, norm_ffn_g, router_w, router_b, exp_w_gate_up,
           exp_b_gate_up, exp_w_down, exp_b_down, final_norm_g):
    d = x_prompt.shape[-1]
    segs = (x_prompt.shape[:2], x_sample.shape[:2])
    y_p, y_s = _encoder(x_prompt.reshape(-1, d), x_sample.reshape(-1, d), segs, norm_mix_g, w_in, hg_lower_bound,
                        hg_norm_g, rw_mu, rw_w0, rw_w2, rw_a0, rw_a2, rw_g2, rw_k_k, rw_k_a, rw_r_k, rw_ln_g,
                        rw_ln_b, w_out, norm_ffn_g, router_w, router_b, exp_w_gate_up, exp_b_gate_up, exp_w_down,
                        exp_b_down, final_norm_g)
    return y_p.reshape(x_prompt.shape), y_s.reshape(x_sample.shape)
```

```python
import functools

import jax
import jax.numpy as jnp
from jax import lax
from jax.experimental import pallas as pl
from jax.experimental.pallas import tpu as pltpu

F32 = jnp.float32
BF16 = jnp.bfloat16

D_MODEL = 1024
HG_WIDTH = 512
HG_HEAD_DIM = 128
HG_HEADS = HG_WIDTH // HG_HEAD_DIM
RW_WIDTH = 512
RW_HEAD_DIM = 64
RW_DECAY_RANK = 64
RW_AAA_RANK = 64
RW_GATE_RANK = 128
HG_COLS = 5 * HG_WIDTH
RW_COLS = 3 * RW_WIDTH + RW_DECAY_RANK + RW_AAA_RANK + RW_GATE_RANK
RW_TAIL = RW_COLS - 3 * RW_WIDTH
N_EXPERTS = 32
TOP_K = 4
D_FF = D_MODEL
SWIGLU_LIMIT = 7.0
SWIGLU_ALPHA = 1.702
NORM_EPS = 1e-5
RW_LN_EPS = 64e-5

V7X_VMEM_LIMIT_BYTES = 56 * 1024 * 1024
SUBLANES = 8
LANES = 128
ROW_TILE = D_MODEL // LANES
assert ROW_TILE == SUBLANES

PROJ_TILE = 256
ROUTER_TILE = 512
HG_CHUNK = 64
HG_BLOCK = 512
RW_CHUNK = 64
RW_GROUP = 256
RW_BLOCK = 512
RW_LANES = 6
MOE_BLOCK = 256
COMBINE_TILE = 128
EXP_CLAMP = 80.0


def _dot(a, b):
    return jnp.dot(a, b, preferred_element_type=F32)


def _dot_nt(a, b):
    return lax.dot_general(a, b, (((1,), (1,)), ((), ())), preferred_element_type=F32)


def _dot_tn(a, b):
    return lax.dot_general(a, b, (((0,), (0,)), ((), ())), preferred_element_type=F32)


def _split(x):
    hi = x.astype(BF16)
    lo = (x - hi.astype(F32)).astype(BF16)
    return hi, lo


def _dot_exact_lhs(a_bf16, x):
    hi, lo = _split(x)
    return _dot(a_bf16, hi) + _dot(a_bf16, lo)


def _dot3(x, w_hi, w_lo):
    hi, lo = _split(x)
    return _dot(hi, w_hi) + _dot(lo, w_hi) + _dot(hi, w_lo)


def _sigmoid(x):
    return 1.0 / (1.0 + jnp.exp(-x))


def _is_seq_boundary(segs, row):
    base = 0
    hit = row == sum(n * t for n, t in segs)
    for n, t in segs:
        inside = jnp.logical_and(row >= base, row < base + n * t)
        hit = jnp.logical_or(hit, jnp.logical_and(inside, lax.rem(row - base, t) == 0))
        base += n * t
    return hit


def _params(vmem=V7X_VMEM_LIMIT_BYTES):
    return pltpu.CompilerParams(dimension_semantics=("arbitrary",), vmem_limit_bytes=vmem)


def _two_array_specs(block, n_a, n_b, shift=0, unit=None):
    rows = block[0]
    per = (unit or rows) // rows
    nba, nbb = n_a // rows, n_b // rows

    def idx_a(i):
        return (jnp.clip(i * per + shift, 0, nba - 1), 0)

    def idx_b(i):
        return (jnp.clip(i * per + shift - nba, 0, nbb - 1), 0)

    return pl.BlockSpec(block, idx_a), pl.BlockSpec(block, idx_b)


def _pick(in_a, ref_a, ref_b):
    return jnp.where(in_a, ref_a[...], ref_b[...])


def _in_proj_body(segs, xpa_ref, xpb_ref, xa_ref, xb_ref, xna_ref, xnb_ref, g_ref, whg_ref, wrw_ref, mu_ref,
                  hg_ref, rw_ref):
    tm = xa_ref.shape[0]
    n_a = segs[0][0] * segs[0][1]
    row0 = pl.program_id(0) * tm
    at_start = _is_seq_boundary(segs, row0)
    at_end = _is_seq_boundary(segs, row0 + tm)
    g = g_ref[...]

    def nrm(x):
        return x * lax.rsqrt(jnp.mean(x * x, axis=-1, keepdims=True) + NORM_EPS) * g

    h = nrm(_pick(row0 < n_a, xa_ref, xb_ref))
    hp = jnp.where(at_start, 0.0, nrm(_pick(row0 - 1 < n_a, xpa_ref, xpb_ref)))
    hn = jnp.where(at_end, 0.0, nrm(_pick(row0 + tm < n_a, xna_ref, xnb_ref)))
    hb = h.astype(BF16)
    for c in range(HG_COLS // HG_WIDTH):
        cs = slice(c * HG_WIDTH, (c + 1) * HG_WIDTH)
        hg_ref[:, cs] = _dot(hb, whg_ref[:, cs]).astype(BF16)
    hs = jnp.concatenate([hp.astype(BF16), hb, hn.astype(BF16)], axis=0)
    p = _dot(hs, wrw_ref[...])
    rows = tm + 2 * SUBLANES
    cur = p[SUBLANES:SUBLANES + tm]
    prev = pltpu.roll(p, 1, 0)[SUBLANES:SUBLANES + tm]
    nxt = pltpu.roll(p, rows - 1, 0)[SUBLANES:SUBLANES + tm]
    mu = mu_ref[...]
    rw_ref[...] = (cur + mu * (0.5 * (prev + nxt) - cur)).astype(BF16)


def _in_proj(x_a, x_b, norm_g, w_hg, w_rw, mu, segs):
    d = x_a.shape[1]
    n_a, n_b = x_a.shape[0], x_b.shape[0]
    n = n_a + n_b
    tm = PROJ_TILE
    return pl.pallas_call(
        functools.partial(_in_proj_body, segs),
        grid=(n // tm,),
        in_specs=[
            *_two_array_specs((SUBLANES, d), n_a, n_b, shift=-1, unit=tm),
            *_two_array_specs((tm, d), n_a, n_b),
            *_two_array_specs((SUBLANES, d), n_a, n_b, shift=tm // SUBLANES, unit=tm),
            pl.BlockSpec((1, d), lambda i: (0, 0)),
            pl.BlockSpec((d, HG_COLS), lambda i: (0, 0)),
            pl.BlockSpec((d, RW_COLS), lambda i: (0, 0)),
            pl.BlockSpec((1, RW_COLS), lambda i: (0, 0)),
        ],
        out_specs=[
            pl.BlockSpec((tm, HG_COLS), lambda i: (i, 0)),
            pl.BlockSpec((tm, RW_COLS), lambda i: (i, 0)),
        ],
        out_shape=[jax.ShapeDtypeStruct((n, HG_COLS), BF16), jax.ShapeDtypeStruct((n, RW_COLS), BF16)],
        compiler_params=_params(),
        name="in_proj",
    )(x_a, x_b, x_a, x_b, x_a, x_b, norm_g, w_hg, w_rw, mu)


def _tri(c, reverse):
    t = lax.broadcasted_iota(jnp.int32, (c, c), 0)
    s = lax.broadcasted_iota(jnp.int32, (c, c), 1)
    return (s >= t) if reverse else (s <= t)


def _hgrn2_body(segs, reverse, nblk, *refs):
    if reverse:
        q_ref, f_ref, i_ref, lb_ref, out_ref, st_ref = refs
    else:
        q_ref, f_ref, i_ref, g_ref, ob_ref, lb_ref, ng_ref, out_ref, st_ref = refs
    tb = q_ref.shape[0]
    c = HG_CHUNK
    step = pl.program_id(0)
    blk = (nblk - 1 - step) if reverse else step
    edge = (blk + 1) * tb if reverse else blk * tb

    @pl.when(_is_seq_boundary(segs, edge))
    def _():
        st_ref[...] = jnp.zeros_like(st_ref)

    lbp = lb_ref[...]
    e0 = jnp.exp(lbp[0:1] - jnp.maximum(lbp[0:1], lbp[1:2]))
    e1 = jnp.exp(lbp[1:2] - jnp.maximum(lbp[0:1], lbp[1:2]))
    lb = e0 / (e0 + e1)
    tri = _tri(c, reverse)
    tri_b = tri.astype(BF16)
    mid = c // 2 if reverse else c // 2 - 1
    last = 0 if reverse else c - 1
    nchunk = tb // c
    order = list(range(nchunk - 1, -1, -1) if reverse else range(nchunk))
    heads = [slice(h * HG_HEAD_DIM, (h + 1) * HG_HEAD_DIM) for h in range(HG_HEADS)]
    gates = {ci: lb + (1.0 - lb) * _sigmoid(f_ref[ci * c:(ci + 1) * c, :].astype(F32)) for ci in order}
    cums = {ci: _dot_exact_lhs(tri_b, jnp.log(gates[ci])) for ci in order}
    cv = {}
    for ci in order:
        rs = slice(ci * c, (ci + 1) * c)
        q = q_ref[rs, :].astype(F32)
        kx = 1.0 - gates[ci]
        cum = cums[ci]
        rho = cum[mid:mid + 1]
        tot = cum[last:last + 1]
        qs = q * _sigmoid(q)
        qh = (qs * jnp.exp(jnp.minimum(cum - rho, EXP_CLAMP))).astype(BF16)
        kh = (kx * jnp.exp(jnp.minimum(rho - cum, EXP_CLAMP))).astype(BF16)
        ke = (kx * jnp.exp(tot - cum)).astype(BF16)
        vv = i_ref[rs, :]
        cv[ci] = dict(qt=(qs * jnp.exp(cum)).astype(BF16), etot=jnp.exp(tot), vv=vv,
                      sc=[jnp.where(tri, _dot_nt(qh[:, hs], kh[:, hs]), 0.0).astype(BF16) for hs in heads],
                      kv=[_dot_tn(vv[:, hs], ke[:, hs]) for hs in heads])
    states = [st_ref[h] for h in range(HG_HEADS)]
    for ci in order:
        rs = slice(ci * c, (ci + 1) * c)
        q = cv[ci]
        outs = []
        for h, hs in enumerate(heads):
            outs.append(_dot(q["sc"][h], q["vv"][:, hs]) + _dot_nt(q["qt"][:, hs], states[h].astype(BF16)))
            states[h] = states[h] * q["etot"][:, hs] + q["kv"][h]
        if ci == order[-1]:
            for h in range(HG_HEADS):
                st_ref[h] = states[h]
        if reverse:
            for h in range(HG_HEADS):
                out_ref[rs, h * HG_HEAD_DIM:(h + 1) * HG_HEAD_DIM] = outs[h].astype(out_ref.dtype)
        else:
            ng = ng_ref[...]
            gg = g_ref[rs, :].astype(F32)
            gate = gg * _sigmoid(gg)
            for h in range(HG_HEADS):
                hs = slice(h * HG_HEAD_DIM, (h + 1) * HG_HEAD_DIM)
                o = outs[h] + ob_ref[rs, hs].astype(F32)
                o = o * lax.rsqrt(jnp.mean(o * o, axis=-1, keepdims=True) + NORM_EPS) * ng[:, hs]
                out_ref[rs, hs] = (o * gate[:, hs]).astype(out_ref.dtype)


def _hgrn2(hg, lower_bound, norm_g, segs):
    n = hg.shape[0]
    tb = HG_BLOCK
    nblk = n // tb
    w = HG_WIDTH
    scratch = [pltpu.VMEM((HG_HEADS, HG_HEAD_DIM, HG_HEAD_DIM), F32)]

    def col(j, rev):
        if rev:
            return pl.BlockSpec((tb, w), lambda i: (nblk - 1 - i, j))
        return pl.BlockSpec((tb, w), lambda i: (i, j))

    o_b = pl.pallas_call(
        functools.partial(_hgrn2_body, segs, True, nblk),
        grid=(nblk,),
        in_specs=[col(0, True), col(2, True), col(3, True), pl.BlockSpec((2, w), lambda i: (0, 0))],
        out_specs=col(0, True),
        out_shape=jax.ShapeDtypeStruct((n, w), BF16),
        scratch_shapes=scratch,
        compiler_params=_params(),
        name="hgrn2_bwd",
    )(hg, hg, hg, lower_bound)
    return pl.pallas_call(
        functools.partial(_hgrn2_body, segs, False, nblk),
        grid=(nblk,),
        in_specs=[col(0, False), col(1, False), col(3, False), col(4, False), col(0, False),
                  pl.BlockSpec((2, w), lambda i: (0, 0)), pl.BlockSpec((1, w), lambda i: (0, 0))],
        out_specs=col(0, False),
        out_shape=jax.ShapeDtypeStruct((n, w), BF16),
        scratch_shapes=scratch,
        compiler_params=_params(),
        name="hgrn2_fwd",
    )(hg, hg, hg, hg, o_b, lower_bound, norm_g)


def _block_diag_mask(n, blk):
    shift = blk.bit_length() - 1
    assert 1 << shift == blk
    r = lax.broadcasted_iota(jnp.int32, (n, n), 0) >> shift
    c = lax.broadcasted_iota(jnp.int32, (n, n), 1) >> shift
    return r == c


def _rwkv7_body(segs, reverse, nblk, *refs):
    if reverse:
        (r_ref, k_ref, v_ref, t_ref, w0_ref, w2h_ref, w2l_ref, a0_ref, a2_ref, kk_ref, ka_ref,
         out_ref, s_ref) = refs
    else:
        (r_ref, k_ref, v_ref, t_ref, ob_ref, w0_ref, w2h_ref, w2l_ref, a0_ref, a2_ref, a0b_ref, a2b_ref,
         g2_ref, kk_ref, ka_ref, rk_ref, lng_ref, lnb_ref, out_ref, s_ref) = refs
    tb = r_ref.shape[0]
    c = RW_CHUNK
    gw = RW_GROUP
    ngroup = RW_WIDTH // gw
    hpg = gw // RW_HEAD_DIM
    step = pl.program_id(0)
    blk = (nblk - 1 - step) if reverse else step
    edge = (blk + 1) * tb if reverse else blk * tb

    @pl.when(_is_seq_boundary(segs, edge))
    def _():
        s_ref[...] = jnp.zeros_like(s_ref)

    tri = _tri(c, reverse)
    tri_b = tri.astype(BF16)
    bd = _block_diag_mask(gw, RW_HEAD_DIM)
    bd_b = bd.astype(BF16)
    tt = lax.broadcasted_iota(jnp.int32, (c, gw), 0)
    ss = lax.broadcasted_iota(jnp.int32, (c, gw), 1) & (c - 1)
    m_strict = (ss > tt) if reverse else (ss < tt)
    m_incl = (ss >= tt) if reverse else (ss <= tt)
    eye = (ss == tt).astype(F32)
    last = 0 if reverse else c - 1

    def blockdiag(x):
        return jnp.concatenate([x.astype(BF16)] * hpg, axis=0) * bd_b

    def headsum(x):
        rows = x.shape[0]
        hi, lo = _split(jnp.concatenate([x[:, g * gw:(g + 1) * gw] for g in range(ngroup)], axis=0))
        s = _dot(jnp.concatenate([hi, lo], axis=0), bd_b)
        s = s[:ngroup * rows] + s[ngroup * rows:]
        return jnp.concatenate([s[g * rows:(g + 1) * rows] for g in range(ngroup)], axis=1)

    k_a = ka_ref[...]
    r = r_ref[...].astype(F32)
    k = k_ref[...].astype(F32)
    vb = v_ref[...]
    lat = t_ref[:, 0:2 * RW_DECAY_RANK].astype(F32)
    latb = lat.astype(BF16)
    wpre = w0_ref[...] + _dot3(jnp.tanh(lat), w2h_ref[...], w2l_ref[...])
    sp = jnp.maximum(-wpre, 0.0) + jnp.log(1.0 + jnp.exp(-jnp.abs(wpre)))
    lw = -jnp.exp(-sp - 0.5)
    a_lr = _sigmoid(a0_ref[...] + _dot(latb, a2_ref[...]))
    kkr = k * kk_ref[...]
    kkn = kkr / jnp.maximum(jnp.sqrt(headsum(kkr * kkr)), 1e-12)
    kd = k * (1.0 + (a_lr - 1.0) * k_a)
    bvec = kkn * a_lr
    nchunk = tb // c
    nsq = c.bit_length() - 1

    order = list(range(nchunk - 1, -1, -1) if reverse else range(nchunk))
    chunk_vals = {}
    pre = {}
    o_parts = {}

    def chunk_prep(ci):
        rs = slice(ci * c, (ci + 1) * c)
        cum = _dot_exact_lhs(tri_b, lw[rs])
        tot = cum[last:last + 1]
        ecn = jnp.exp(-cum)
        eend = jnp.exp(tot - cum)
        chunk_vals[ci] = dict(rt=r[rs] * jnp.exp(cum), at=-kkn[rs] * jnp.exp(cum - lw[rs]),
                              bt=bvec[rs] * ecn, kt=kd[rs] * ecn, bh=bvec[rs] * eend, kh=kd[rs] * eend,
                              etot=jnp.exp(tot))

    def independent_task(ci, g):
        if ci not in chunk_vals:
            chunk_prep(ci)
            yield
        cv = chunk_vals[ci]
        rs = slice(ci * c, (ci + 1) * c)
        gs = slice(g * gw, (g + 1) * gw)
        ar = jnp.concatenate([cv["at"][:, gs], cv["rt"][:, gs]], axis=0).astype(BF16)
        g1 = _dot_nt(ar, blockdiag(cv["bt"][:, gs]))
        yield
        g2 = _dot_nt(ar, blockdiag(cv["kt"][:, gs]))
        yield
        n_ab = jnp.where(m_strict, g1[:c], 0.0)
        n_ak = jnp.where(m_strict, g2[:c], 0.0)
        m_rb = jnp.where(m_incl, g1[c:], 0.0)
        m_rk = jnp.where(m_incl, g2[c:], 0.0)
        vbd = blockdiag(vb[rs, gs])
        nakv = _dot(n_ak.astype(BF16), vbd)
        yield
        mrkv = _dot(m_rk.astype(BF16), vbd)
        yield
        tinv = eye + n_ab
        pw = n_ab
        for _ in range(1, nsq):
            pw = _dot(pw.astype(BF16), blockdiag(pw))
            yield
            tinv = tinv + _dot(pw.astype(BF16), blockdiag(tinv))
            yield
        pre[ci, g] = dict(ar=ar, tinv=tinv.astype(BF16), etot=cv["etot"][:, gs], nakv=nakv, mrkv=mrkv,
                          m_rb=m_rb.astype(BF16),
                          bk=jnp.concatenate([cv["bh"][:, gs], cv["kh"][:, gs]], axis=0).astype(BF16))

    def recurrent_task(g):
        gs = slice(g * gw, (g + 1) * gw)
        s_g = s_ref[g]
        for ci in order:
            while (ci, g) not in pre:
                yield
            q = pre[ci, g]
            ars = _dot_nt(q["ar"], s_g.astype(BF16))
            yield
            u = _dot(q["tinv"], blockdiag(ars[:c] + q["nakv"]))
            yield
            o_parts[ci, g] = ars[c:] + _dot(q["m_rb"], blockdiag(u)) + q["mrkv"]
            yield
            uv = jnp.concatenate([u.astype(BF16), vb[ci * c:(ci + 1) * c, gs]], axis=0)
            s_g = s_g * q["etot"] + jnp.where(bd, _dot_tn(uv, q["bk"]), 0.0)
            yield
        s_ref[g] = s_g

    waiting = [independent_task(ci, g) for ci in order for g in range(ngroup)]
    running = [recurrent_task(g) for g in range(ngroup)]
    lanes = []
    while running or lanes or waiting:
        while waiting and len(lanes) < RW_LANES:
            lanes.append(waiting.pop(0))
        for group in (running, lanes):
            for task in list(group):
                try:
                    next(task)
                except StopIteration:
                    group.remove(task)
    o_chunks = [[o_parts[ci, g] for g in range(ngroup)] for ci in range(nchunk)]
    o = jnp.concatenate([jnp.concatenate(oc, axis=1) for oc in o_chunks], axis=0)
    if reverse:
        out_ref[...] = o.astype(out_ref.dtype)
    else:
        o = o + ob_ref[...].astype(F32)
        inv_n = 1.0 / RW_HEAD_DIM
        mean = headsum(o) * inv_n
        d = o - mean
        var = headsum(d * d) * inv_n
        o = d * lax.rsqrt(var + RW_LN_EPS) * lng_ref[...] + lnb_ref[...]
        a_lr_b = _sigmoid(a0b_ref[...] + _dot(latb, a2b_ref[...]))
        kd_b = k * (1.0 + (a_lr_b - 1.0) * k_a)
        bonus = headsum(r * (kd + kd_b) * rk_ref[...]) * vb.astype(F32)
        gate = _dot(_sigmoid(t_ref[:, 2 * RW_DECAY_RANK:].astype(F32)).astype(BF16), g2_ref[...])
        out_ref[...] = ((o + bonus) * gate).astype(out_ref.dtype)


def _rwkv7(rw, p, segs):
    n = rw.shape[0]
    tb = RW_BLOCK
    nblk = n // tb
    w = RW_WIDTH
    scratch = [pltpu.VMEM((w // RW_GROUP, RW_GROUP, RW_GROUP), F32)]

    def rows(j, width, rev):
        if rev:
            return pl.BlockSpec((tb, width), lambda i: (nblk - 1 - i, j))
        return pl.BlockSpec((tb, width), lambda i: (i, j))

    def full(a):
        return pl.BlockSpec(a.shape, lambda i: (0,) * a.ndim)

    tail_blk = (3 * w) // RW_TAIL
    bwd_params = [p["w0_b"], p["w2h_b"], p["w2l_b"], p["a0_b"], p["a2_b"], p["k_k"], p["k_a"]]
    o_b = pl.pallas_call(
        functools.partial(_rwkv7_body, segs, True, nblk),
        grid=(nblk,),
        in_specs=[rows(0, w, True), rows(1, w, True), rows(2, w, True), rows(tail_blk, RW_TAIL, True)]
        + [full(a) for a in bwd_params],
        out_specs=rows(0, w, True),
        out_shape=jax.ShapeDtypeStruct((n, w), BF16),
        scratch_shapes=scratch,
        compiler_params=_params(),
        name="rwkv7_bwd",
    )(rw, rw, rw, rw, *bwd_params)
    fwd_params = [p["w0_f"], p["w2h_f"], p["w2l_f"], p["a0_f"], p["a2_f"], p["a0_b"], p["a2_b"], p["g2"],
                  p["k_k"], p["k_a"], p["r_k"], p["ln_g"], p["ln_b"]]
    return pl.pallas_call(
        functools.partial(_rwkv7_body, segs, False, nblk),
        grid=(nblk,),
        in_specs=[rows(0, w, False), rows(1, w, False), rows(2, w, False), rows(tail_blk, RW_TAIL, False),
                  rows(0, w, False)] + [full(a) for a in fwd_params],
        out_specs=rows(0, w, False),
        out_shape=jax.ShapeDtypeStruct((n, w), BF16),
        scratch_shapes=scratch,
        compiler_params=_params(),
        name="rwkv7_fwd",
    )(rw, rw, rw, rw, o_b, *fwd_params)


def _store_row_tiles(ref, val):
    rows = val.shape[0]
    for j in range(ROW_TILE):
        ref[pl.ds(j, rows, stride=ROW_TILE), :] = val[:, j * LANES:(j + 1) * LANES]


def _load_row_tiles(ref, rows):
    return jnp.concatenate([ref[pl.ds(j, rows, stride=ROW_TILE), :] for j in range(ROW_TILE)], axis=1)


def _out_router_body(n_a, xa_ref, xb_ref, yh_ref, yr_ref, woa_ref, wob_ref, g_ref, rwh_ref, rwl_ref, rb_ref,
                     x2_ref, h2_ref, idx_ref, gate_ref, rank_ref, cnt_ref, carry_ref):
    tm = xa_ref.shape[0]

    @pl.when(pl.program_id(0) == 0)
    def _():
        carry_ref[...] = jnp.zeros_like(carry_ref)

    x = _pick(pl.program_id(0) * tm < n_a, xa_ref, xb_ref)
    x2 = x + _dot(yh_ref[...], woa_ref[...]) + _dot(yr_ref[...], wob_ref[...])
    x2_ref[...] = x2
    h2 = x2 * lax.rsqrt(jnp.mean(x2 * x2, axis=-1, keepdims=True) + NORM_EPS) * g_ref[...]
    _store_row_tiles(h2_ref, h2)
    hi, lo = _split(h2)
    logits = (_dot_nt(rwh_ref[...], hi) + _dot_nt(rwh_ref[...], lo) + _dot_nt(rwl_ref[...], hi)) + rb_ref[...]
    ids = lax.broadcasted_iota(jnp.int32, logits.shape, 0)
    vals = logits
    tops, picks = [], []
    for _ in range(TOP_K):
        m = jnp.max(vals, axis=0, keepdims=True)
        pick = jnp.min(jnp.where(vals == m, ids, N_EXPERTS), axis=0, keepdims=True)
        vals = jnp.where(ids == pick, -jnp.inf, vals)
        tops.append(m)
        picks.append(pick)
    es = [jnp.exp(t - tops[0]) for t in tops]
    den = es[0] + es[1] + es[2] + es[3]
    idx_ref[...] = jnp.concatenate(picks, axis=0)
    gate_ref[...] = jnp.concatenate([e / den for e in es], axis=0)
    onehots = [(ids == pick).astype(F32) for pick in picks]
    member = (onehots[0] + onehots[1]) + (onehots[2] + onehots[3])
    t0 = lax.broadcasted_iota(jnp.int32, (tm, tm), 0)
    t1 = lax.broadcasted_iota(jnp.int32, (tm, tm), 1)
    earlier = _dot(member.astype(BF16), (t0 < t1).astype(BF16))
    base = carry_ref[:, 0:1] + earlier
    rank_ref[...] = jnp.concatenate([jnp.sum(oh * base, axis=0, keepdims=True) for oh in onehots],
                                    axis=0).astype(jnp.int32)
    carry = carry_ref[...] + jnp.sum(member, axis=1, keepdims=True)
    carry_ref[...] = carry
    cnt_ref[...] = carry


def _out_router(x_a, x_b, y_hg, y_rw, wo_a, wo_b, norm_g, rw_hi, rw_lo, rb):
    d = x_a.shape[1]
    n_a, n_b = x_a.shape[0], x_b.shape[0]
    n = n_a + n_b
    tm = ROUTER_TILE

    def full(a):
        return pl.BlockSpec(a.shape, lambda i: (0,) * a.ndim)

    return pl.pallas_call(
        functools.partial(_out_router_body, n_a),
        grid=(n // tm,),
        in_specs=[*_two_array_specs((tm, d), n_a, n_b),
                  pl.BlockSpec((tm, HG_WIDTH), lambda i: (i, 0)),
                  pl.BlockSpec((tm, RW_WIDTH), lambda i: (i, 0)),
                  full(wo_a), full(wo_b), full(norm_g), full(rw_hi), full(rw_lo), full(rb)],
        out_specs=[pl.BlockSpec((tm, d), lambda i: (i, 0)),
                   pl.BlockSpec((tm * ROW_TILE, LANES), lambda i: (i, 0)),
                   pl.BlockSpec((TOP_K, tm), lambda i: (0, i)),
                   pl.BlockSpec((TOP_K, tm), lambda i: (0, i)),
                   pl.BlockSpec((TOP_K, tm), lambda i: (0, i)),
                   pl.BlockSpec((N_EXPERTS, LANES), lambda i: (0, 0))],
        out_shape=[jax.ShapeDtypeStruct((n, d), F32), jax.ShapeDtypeStruct((n * ROW_TILE, LANES), F32),
                   jax.ShapeDtypeStruct((TOP_K, n), jnp.int32), jax.ShapeDtypeStruct((TOP_K, n), F32),
                   jax.ShapeDtypeStruct((TOP_K, n), jnp.int32),
                   jax.ShapeDtypeStruct((N_EXPERTS, LANES), F32)],
        scratch_shapes=[pltpu.VMEM((N_EXPERTS, LANES), F32)],
        compiler_params=_params(),
        name="out_router",
    )(x_a, x_b, y_hg, y_rw, wo_a, wo_b, norm_g, rw_hi, rw_lo, rb)


def _tile_copy(src_hbm, row, dst_vmem, slot, sem):
    return pltpu.make_async_copy(src_hbm.at[pl.ds(pl.multiple_of(row * ROW_TILE, ROW_TILE), ROW_TILE)],
                                 dst_vmem.at[pl.ds(slot * ROW_TILE, ROW_TILE)], sem)


def _moe_body(be_ref, nu_ref, tok_ref, tokn_ref, h_hbm, wgu_ref, bgu_ref, wdn_ref, bdn_ref, out_ref,
              xbuf, sem):
    b = pl.program_id(0)
    nu = nu_ref[0]
    blk = MOE_BLOCK
    cur = lax.rem(b, 2)

    def gather(toks, s):
        for r in range(blk):
            _tile_copy(h_hbm, toks[0, 0, r], xbuf.at[s], r, sem.at[s]).start()

    def drain(s):
        for r in range(blk):
            _tile_copy(h_hbm, 0, xbuf.at[s], r, sem.at[s]).wait()

    @pl.when(jnp.logical_and(b == 0, nu > 0))
    def _():
        gather(tok_ref, 0)

    @pl.when(b < nu)
    def _():
        drain(cur)
        gather(tokn_ref, 1 - cur)
        x = _load_row_tiles(xbuf.at[cur], blk).astype(BF16)
        gu = _dot(x, wgu_ref[0]) + bgu_ref[0]
        glu = jnp.minimum(gu[:, :D_FF], SWIGLU_LIMIT)
        lin = jnp.clip(gu[:, D_FF:], -SWIGLU_LIMIT, SWIGLU_LIMIT)
        act = glu * _sigmoid(SWIGLU_ALPHA * glu) * (lin + 1.0)
        _store_row_tiles(out_ref, _dot(act.astype(BF16), wdn_ref[0]) + bdn_ref[0])

        @pl.when(b == nu - 1)
        def _():
            drain(1 - cur)

    @pl.when(b >= nu)
    def _():
        out_ref[...] = jnp.zeros_like(out_ref)


def _moe(h2_tiles, blk_expert, n_used, row_tok, w_gu, b_gu, w_dn, b_dn):
    d = D_MODEL
    blk = MOE_BLOCK
    nblk = row_tok.shape[0]
    grid_spec = pltpu.PrefetchScalarGridSpec(
        num_scalar_prefetch=2,
        grid=(nblk,),
        in_specs=[
            pl.BlockSpec((1, 1, blk), lambda b, be, nu: (b, 0, 0), memory_space=pltpu.SMEM),
            pl.BlockSpec((1, 1, blk), lambda b, be, nu: (jnp.minimum(b + 1, nblk - 1), 0, 0),
                         memory_space=pltpu.SMEM),
            pl.BlockSpec(memory_space=pl.ANY),
            pl.BlockSpec((1, d, 2 * D_FF), lambda b, be, nu: (be[b], 0, 0)),
            pl.BlockSpec((1, 1, 2 * D_FF), lambda b, be, nu: (be[b], 0, 0)),
            pl.BlockSpec((1, D_FF, d), lambda b, be, nu: (be[b], 0, 0)),
            pl.BlockSpec((1, 1, d), lambda b, be, nu: (be[b], 0, 0)),
        ],
        out_specs=pl.BlockSpec((blk * ROW_TILE, LANES), lambda b, be, nu: (b, 0)),
        scratch_shapes=[pltpu.VMEM((2, blk * ROW_TILE, LANES), F32), pltpu.SemaphoreType.DMA((2,))],
    )
    return pl.pallas_call(
        _moe_body,
        grid_spec=grid_spec,
        out_shape=jax.ShapeDtypeStruct((nblk * blk * ROW_TILE, LANES), F32),
        compiler_params=_params(),
        name="moe_experts",
    )(blk_expert, n_used, row_tok, row_tok, h2_tiles, w_gu, b_gu, w_dn, b_dn)


def _combine_body(n_a, pos_ref, posn_ref, x2_ref, gate_ref, g_ref, yb_hbm, outa_ref, outb_ref, ybuf, sem):
    tm = x2_ref.shape[0]
    i = pl.program_id(0)
    cur = lax.rem(i, 2)

    def gather(pos, s):
        for t in range(tm):
            for k in range(TOP_K):
                _tile_copy(yb_hbm, pos[k, t], ybuf.at[s, k], t, sem.at[s]).start(priority=k % 2)

    def drain(s):
        for t in range(tm):
            for k in range(TOP_K):
                _tile_copy(yb_hbm, 0, ybuf.at[s, k], t, sem.at[s]).wait()

    @pl.when(i == 0)
    def _():
        gather(pos_ref, 0)

    @pl.when(i + 1 < pl.num_programs(0))
    def _():
        gather(posn_ref, 1 - cur)

    drain(cur)
    rr = lax.broadcasted_iota(jnp.int32, (tm, tm), 0)
    cc = lax.broadcasted_iota(jnp.int32, (tm, tm), 1)
    x = x2_ref[...]
    for k in range(TOP_K):
        gcol = jnp.sum(jnp.where(rr == cc, gate_ref[k:k + 1, :], 0.0), axis=1, keepdims=True)
        x = x + gcol * _load_row_tiles(ybuf.at[cur, k], tm)
    y = x * lax.rsqrt(jnp.mean(x * x, axis=-1, keepdims=True) + NORM_EPS) * g_ref[...]

    @pl.when(i * tm < n_a)
    def _():
        outa_ref[...] = y

    @pl.when(i * tm >= n_a)
    def _():
        outb_ref[...] = y


def _combine(pos_t, gate_t, x2, final_g, yb_tiles, n_a):
    n, d = x2.shape
    tm = COMBINE_TILE
    last = n // tm - 1
    out_a, out_b = _two_array_specs((tm, d), n_a, n - n_a)
    return pl.pallas_call(
        functools.partial(_combine_body, n_a),
        grid=(n // tm,),
        in_specs=[pl.BlockSpec((TOP_K, tm), lambda i: (0, i), memory_space=pltpu.SMEM),
                  pl.BlockSpec((TOP_K, tm), lambda i: (0, jnp.minimum(i + 1, last)), memory_space=pltpu.SMEM),
                  pl.BlockSpec((tm, d), lambda i: (i, 0)),
                  pl.BlockSpec((TOP_K, tm), lambda i: (0, i)),
                  pl.BlockSpec((1, d), lambda i: (0, 0)),
                  pl.BlockSpec(memory_space=pl.ANY)],
        out_specs=[out_a, out_b],
        out_shape=[jax.ShapeDtypeStruct((n_a, d), F32), jax.ShapeDtypeStruct((n - n_a, d), F32)],
        scratch_shapes=[pltpu.VMEM((2, TOP_K, tm * ROW_TILE, LANES), F32), pltpu.SemaphoreType.DMA((2,))],
        compiler_params=_params(),
        name="moe_combine",
    )(pos_t, pos_t, x2, gate_t, final_g, yb_tiles)


def _routing_tables(idx_t, rank_t, counts):
    n = idx_t.shape[1]
    n_assign = n * TOP_K
    blk = MOE_BLOCK
    counts = counts.astype(jnp.int32)
    padded = (counts + blk - 1) // blk * blk
    grp_start = jnp.cumsum(counts) - counts
    pad_end = jnp.cumsum(padded)
    pad_start = pad_end - padded
    onehot = idx_t[..., None] == jnp.arange(N_EXPERTS, dtype=jnp.int32)
    pos_t = rank_t + jnp.sum(jnp.where(onehot, pad_start, 0), axis=-1)
    keys = jnp.sort((idx_t * n + jnp.arange(n, dtype=jnp.int32)[None]).reshape(-1))
    nblk = (n_assign + N_EXPERTS * (blk - 1)) // blk + 1
    blk_start = jnp.arange(nblk, dtype=jnp.int32) * blk
    blk_expert = jnp.minimum(jnp.sum((pad_end[None, :] <= blk_start[:, None]).astype(jnp.int32), axis=1),
                             N_EXPERTS - 1)
    within = (blk_start - pad_start[blk_expert])[:, None] + jnp.arange(blk, dtype=jnp.int32)[None]
    valid = within < counts[blk_expert][:, None]
    src = jnp.clip(grp_start[blk_expert][:, None] + within, 0, n_assign - 1)
    row_tok = jnp.where(valid, keys[src] - blk_expert[:, None] * n, 0)
    n_used = (pad_end[-1] // blk).astype(jnp.int32).reshape(1)
    return blk_expert, n_used, row_tok.reshape(nblk, 1, blk), pos_t


def _pad_rows(w, top, total):
    return jnp.zeros((total, w.shape[1]), w.dtype).at[top:top + w.shape[0]].set(w)


def _encoder(x_a, x_b, segs, norm_mix_g, w_in, hg_lower_bound, hg_norm_g, rw_mu, rw_w0, rw_w2, rw_a0, rw_a2, rw_g2,
             rw_k_k, rw_k_a, rw_r_k, rw_ln_g, rw_ln_b, w_out, norm_ffn_g, router_w, router_b, exp_w_gate_up,
             exp_b_gate_up, exp_w_down, exp_b_down, final_norm_g):
    w_in_b = w_in[0].astype(BF16)
    hg, rw = _in_proj(x_a, x_b, norm_mix_g, w_in_b[:, :HG_COLS], w_in_b[:, HG_COLS:], rw_mu, segs)
    y_hg = _hgrn2(hg, hg_lower_bound, hg_norm_g, segs)
    lat = 2 * RW_DECAY_RANK
    p = {"k_k": rw_k_k, "k_a": rw_k_a, "r_k": rw_r_k, "ln_g": rw_ln_g, "ln_b": rw_ln_b,
         "g2": rw_g2[0].astype(BF16)}
    for d, tag in ((0, "f"), (1, "b")):
        w2 = _pad_rows(rw_w2[0, d], 0, lat)
        p["w2h_" + tag], p["w2l_" + tag] = _split(w2)
        p["a2_" + tag] = _pad_rows(rw_a2[0, d], RW_DECAY_RANK, lat).astype(BF16)
        p["w0_" + tag] = rw_w0[0, d][None]
        p["a0_" + tag] = rw_a0[0, d][None]
    y_rw = _rwkv7(rw, p, segs)
    wo = w_out[0].astype(BF16)
    rw_hi, rw_lo = _split(router_w[0].T)
    x2, h2, idx_t, gate_t, rank_t, counts = _out_router(x_a, x_b, y_hg, y_rw, wo[:HG_WIDTH], wo[HG_WIDTH:],
                                                        norm_ffn_g, rw_hi, rw_lo, router_b[0][:, None])
    blk_expert, n_used, row_tok, pos_t = _routing_tables(idx_t, rank_t, counts[:, 0])
    yb = _moe(h2, blk_expert, n_used, row_tok, exp_w_gate_up[0].astype(BF16),
              exp_b_gate_up[0][:, None, :], exp_w_down[0].astype(BF16), exp_b_down[0][:, None, :])
    return _combine(pos_t, gate_t, x2, final_norm_g[None], yb, x_a.shape[0])


def kernel(x_prompt, x_sample, norm_mix_g, w_in, hg_lower_bound, hg_norm_g, rw_mu, rw_w0, rw_w2, rw_a0, rw_a2,
           rw_g2, rw_k_k, rw_k_a, rw_r_k, rw_ln_g, rw_ln_b, w_out, norm_ffn_g, router_w, router_b, exp_w_gate_up,
           exp_b_gate_up, exp_w_down, exp_b_down, final_norm_g):
    d = x_prompt.shape[-1]
    segs = (x_prompt.shape[:2], x_sample.shape[:2])
    y_p, y_s = _encoder(x_prompt.reshape(-1, d), x_sample.reshape(-1, d), segs, norm_mix_g, w_in, hg_lower_bound,
                        hg_norm_g, rw_mu, rw_w0, rw_w2, rw_a0, rw_a2, rw_g2, rw_k_k, rw_k_a, rw_r_k, rw_ln_g,
                        rw_ln_b, w_out, norm_ffn_g, router_w, router_b, exp_w_gate_up, exp_b_gate_up, exp_w_down,
                        exp_b_down, final_norm_g)
    return y_p.reshape(x_prompt.shape), y_s.reshape(x_sample.shape)
```
